```python
import math
import jax, jax.numpy as jnp
from jax import lax
import numpy as np

D_MODEL = 2048
BATCH = 8
SEQ = 2048
DEPTH = 1
DEC_BATCH = 8
DEC_SEQ = 32
PAST_LEN = 2048

CHUNK = 64
D_MIX = D_MODEL
CONV_DIM = D_MIX // 2
CONV_WIDTH = 31
CONV_STATE = CONV_WIDTH - 1
SB_HEADS = 8
SB_HEAD_DIM = (D_MIX - CONV_DIM) // SB_HEADS
SB_DIM = SB_HEADS * SB_HEAD_DIM
SB_BLOCK = 128
MEM_TOKENS = 256
XATTN_HEADS = 4
XATTN_DIM = D_MODEL // 2
XATTN_HEAD_DIM = XATTN_DIM // XATTN_HEADS
D_FF = ((8 * D_MODEL // 3) + 255) // 256 * 256
IN_COLS = 2 * CONV_DIM + 3 * SB_DIM
LN_EPS = 1e-5
DEEPNORM_ALPHA = (2.0 * DEPTH) ** 0.25
DEEPNORM_BETA = (8.0 * DEPTH) ** -0.25

kernel_name = 'hybrid_conformer_stickbreak_stream_step'


def layer_norm(x, g, b):
    xf = x.astype(jnp.float32)
    mu = jnp.mean(xf, axis=-1, keepdims=True)
    var = jnp.mean(jnp.square(xf - mu), axis=-1, keepdims=True)
    y = (xf - mu) * lax.rsqrt(var + LN_EPS) * g.astype(jnp.float32) + b.astype(jnp.float32)
    return y.astype(x.dtype)


def post_norm(x, sub, g, b):
    return layer_norm(DEEPNORM_ALPHA * x + sub, g, b)


def swiglu_ffn(x, w_gate, w_up, w_down):
    return (jax.nn.silu(x @ w_gate) * (x @ w_up)) @ w_down


def conv_module(u, prev, conv_w, conv_b, ln_g, ln_b):
    a, gate = jnp.split(u, 2, axis=-1)
    h = a * jax.nn.sigmoid(gate)
    hp = jnp.concatenate([prev.astype(h.dtype), h], axis=1)
    y = lax.conv_general_dilated(
        hp, conv_w[:, None, :].astype(h.dtype), window_strides=(1,), padding='VALID',
        dimension_numbers=('NWC', 'WIO', 'NWC'), feature_group_count=CONV_DIM) + conv_b
    y = jax.nn.silu(layer_norm(y, ln_g, ln_b))
    return y, hp[:, -CONV_STATE:]


def sb_block(q, k, v, q_pos, k_pos):
    z = jnp.einsum('bqhd,bkhd->bhqk', q.astype(jnp.float32), k.astype(jnp.float32)) * (SB_HEAD_DIM ** -0.5)
    mask = k_pos[None, :] < q_pos[:, None]
    log_stay = jnp.where(mask, jax.nn.log_sigmoid(-z), 0.0)
    after = lax.cumsum(log_stay, axis=3, reverse=True) - log_stay
    a = jnp.where(mask, jnp.exp(jax.nn.log_sigmoid(z) + after), 0.0)
    return jnp.einsum('bhqk,bkhd->bqhd', a, v.astype(jnp.float32)).astype(q.dtype)


def sb_attention(q, k, v, q_pos, k_pos):
    b, t = q.shape[0], q.shape[1]
    if t <= SB_BLOCK:
        return sb_block(q, k, v, q_pos, k_pos)
    nb = t // SB_BLOCK
    qb = jnp.moveaxis(q.reshape(b, nb, SB_BLOCK, SB_HEADS, SB_HEAD_DIM), 1, 0)
    pb = q_pos.reshape(nb, SB_BLOCK)
    ob = lax.map(lambda qp: sb_block(qp[0], k, v, qp[1], k_pos), (qb, pb))
    return jnp.moveaxis(ob, 0, 1).reshape(b, t, SB_HEADS, SB_HEAD_DIM)


def memory_kv(mem, wk, wv):
    b, m, _ = mem.shape
    k = (mem @ wk).reshape(b, m, XATTN_HEADS, XATTN_HEAD_DIM)
    v = (mem @ wv).reshape(b, m, XATTN_HEADS, XATTN_HEAD_DIM)
    return k, v


def cross_attention(x, mem_k, mem_v, wq, wo):
    b, t, _ = x.shape
    q = (x @ wq).reshape(b, t, XATTN_HEADS, XATTN_HEAD_DIM)
    s = jnp.einsum('bqhd,bmhd->bhqm', q.astype(jnp.float32), mem_k.astype(jnp.float32)) * (XATTN_HEAD_DIM ** -0.5)
    p = jax.nn.softmax(s, axis=-1)
    o = jnp.einsum('bhqm,bmhd->bqhd', p, mem_v.astype(jnp.float32)).reshape(b, t, XATTN_DIM).astype(x.dtype)
    return o @ wo


def encoder_layer(x, conv_prev, k_past, v_past, mem_k, mem_v, f1g, f1u, f1d, ln1g, ln1b, w_in, conv_w, conv_b,
                  cln_g, cln_b, w_out, ln2g, ln2b, xq, xo, ln3g, ln3b, f2g, f2u, f2d, ln4g, ln4b):
    b, t, _ = x.shape
    p_len = k_past.shape[1]
    x = post_norm(x, 0.5 * swiglu_ffn(x, f1g, f1u, f1d), ln1g, ln1b)
    proj = x @ w_in
    u_conv, q, k, v = jnp.split(proj, [2 * CONV_DIM, 2 * CONV_DIM + SB_DIM, 2 * CONV_DIM + 2 * SB_DIM], axis=-1)
    y_conv, conv_new = conv_module(u_conv, conv_prev, conv_w, conv_b, cln_g, cln_b)
    q = q.reshape(b, t, SB_HEADS, SB_HEAD_DIM)
    k = k.reshape(b, t, SB_HEADS, SB_HEAD_DIM)
    v = v.reshape(b, t, SB_HEADS, SB_HEAD_DIM)
    k_all = jnp.concatenate([k_past.astype(k.dtype), k], axis=1)
    v_all = jnp.concatenate([v_past.astype(v.dtype), v], axis=1)
    q_pos = p_len + jnp.arange(t, dtype=jnp.int32)
    k_pos = jnp.arange(p_len + t, dtype=jnp.int32)
    y_sb = sb_attention(q, k_all, v_all, q_pos, k_pos).reshape(b, t, SB_DIM)
    mix = jnp.concatenate([y_conv, y_sb], axis=-1) @ w_out
    x = post_norm(x, mix, ln2g, ln2b)
    x = post_norm(x, cross_attention(x, mem_k, mem_v, xq, xo), ln3g, ln3b)
    x = post_norm(x, 0.5 * swiglu_ffn(x, f2g, f2u, f2d), ln4g, ln4b)
    return x, conv_new, k, v


def setup_inputs(seed: int = 0) -> dict:
    key = jax.random.key(seed)
    ks = iter(jax.random.split(key, 48))

    def nrm(shape, scale=1.0):
        return jax.random.normal(next(ks), shape, jnp.float32) * scale

    def dense(fan_in, fan_out, scale=1.0):
        return nrm((DEPTH, fan_in, fan_out), scale * fan_in ** -0.5)

    def gain(n):
        return 1.0 + nrm((DEPTH, n), 0.02)

    def bias(n):
        return nrm((DEPTH, n), 0.02)

    in_scale = jnp.concatenate([jnp.ones((2 * CONV_DIM + 2 * SB_DIM,), jnp.float32),
                                jnp.full((SB_DIM,), DEEPNORM_BETA, jnp.float32)])
    return {
        'x_prompt': nrm((BATCH, SEQ, D_MODEL)),
        'x_sample': nrm((DEC_BATCH, DEC_SEQ, D_MODEL)),
        'mem_prompt': nrm((BATCH, MEM_TOKENS, D_MODEL)),
        'cache_conv': nrm((DEPTH, DEC_BATCH, CONV_STATE, CONV_DIM), 0.5),
        'cache_sb_k': nrm((DEPTH, DEC_BATCH, PAST_LEN, SB_HEADS, SB_HEAD_DIM)),
        'cache_sb_v': nrm((DEPTH, DEC_BATCH, PAST_LEN, SB_HEADS, SB_HEAD_DIM), DEEPNORM_BETA),
        'cache_mem_k': nrm((DEPTH, DEC_BATCH, MEM_TOKENS, XATTN_HEADS, XATTN_HEAD_DIM)),
        'cache_mem_v': nrm((DEPTH, DEC_BATCH, MEM_TOKENS, XATTN_HEADS, XATTN_HEAD_DIM), DEEPNORM_BETA),
        'ffn1_w_gate': dense(D_MODEL, D_FF),
        'ffn1_w_up': dense(D_MODEL, D_FF, DEEPNORM_BETA),
        'ffn1_w_down': dense(D_FF, D_MODEL, DEEPNORM_BETA),
        'ln1_g': gain(D_MODEL),
        'ln1_b': bias(D_MODEL),
        'w_in': dense(D_MODEL, IN_COLS) * in_scale,
        'conv_w': nrm((DEPTH, CONV_WIDTH, CONV_DIM), CONV_WIDTH ** -0.5),
        'conv_b': bias(CONV_DIM),
        'conv_ln_g': gain(CONV_DIM),
        'conv_ln_b': bias(CONV_DIM),
        'w_out': dense(D_MIX, D_MODEL, DEEPNORM_BETA),
        'ln2_g': gain(D_MODEL),
        'ln2_b': bias(D_MODEL),
        'xattn_wq': dense(D_MODEL, XATTN_DIM),
        'xattn_wk': dense(D_MODEL, XATTN_DIM),
        'xattn_wv': dense(D_MODEL, XATTN_DIM, DEEPNORM_BETA),
        'xattn_wo': dense(XATTN_DIM, D_MODEL, DEEPNORM_BETA),
        'ln3_g': gain(D_MODEL),
        'ln3_b': bias(D_MODEL),
        'ffn2_w_gate': dense(D_MODEL, D_FF),
        'ffn2_w_up': dense(D_MODEL, D_FF, DEEPNORM_BETA),
        'ffn2_w_down': dense(D_FF, D_MODEL, DEEPNORM_BETA),
        'ln4_g': gain(D_MODEL),
        'ln4_b': bias(D_MODEL),
    }


def reference(x_prompt, x_sample, mem_prompt, cache_conv, cache_sb_k, cache_sb_v, cache_mem_k, cache_mem_v,
              ffn1_w_gate, ffn1_w_up, ffn1_w_down, ln1_g, ln1_b, w_in, conv_w, conv_b, conv_ln_g, conv_ln_b,
              w_out, ln2_g, ln2_b, xattn_wq, xattn_wk, xattn_wv, xattn_wo, ln3_g, ln3_b,
              ffn2_w_gate, ffn2_w_up, ffn2_w_down, ln4_g, ln4_b):
    xp, xs = x_prompt, x_sample
    b_p = x_prompt.shape[0]
    conv_p, k_p, v_p, mk_p, mv_p = [], [], [], [], []
    conv_s, k_s, v_s = [], [], []
    for l in range(DEPTH):
        w = (ffn1_w_gate[l], ffn1_w_up[l], ffn1_w_down[l], ln1_g[l], ln1_b[l], w_in[l], conv_w[l], conv_b[l],
             conv_ln_g[l], conv_ln_b[l], w_out[l], ln2_g[l], ln2_b[l], xattn_wq[l], xattn_wo[l], ln3_g[l], ln3_b[l],
             ffn2_w_gate[l], ffn2_w_up[l], ffn2_w_down[l], ln4_g[l], ln4_b[l])
        mk, mv = memory_kv(mem_prompt, xattn_wk[l], xattn_wv[l])
        zero_conv = jnp.zeros((b_p, CONV_STATE, CONV_DIM), xp.dtype)
        zero_kv = jnp.zeros((b_p, 0, SB_HEADS, SB_HEAD_DIM), xp.dtype)
        xp, cp, kp, vp = encoder_layer(xp, zero_conv, zero_kv, zero_kv, mk, mv, *w)
        conv_p.append(cp); k_p.append(kp); v_p.append(vp); mk_p.append(mk); mv_p.append(mv)
        xs, cs, ks_, vs_ = encoder_layer(xs, cache_conv[l], cache_sb_k[l], cache_sb_v[l],
                                         cache_mem_k[l], cache_mem_v[l], *w)
        conv_s.append(cs); k_s.append(ks_); v_s.append(vs_)
    return (xp, xs, jnp.stack(conv_p), jnp.stack(k_p), jnp.stack(v_p), jnp.stack(mk_p), jnp.stack(mv_p),
            jnp.stack(conv_s), jnp.stack(k_s), jnp.stack(v_s))
```

```python
import functools

import jax
import jax.numpy as jnp
from jax import lax
from jax.experimental import pallas as pl
from jax.experimental.pallas import tpu as pltpu

F32 = jnp.float32
BF16 = jnp.bfloat16

LN_EPS = 1e-5
CONV_WIDTH = 31
CONV_STATE = CONV_WIDTH - 1
SB_HEADS = 8
XATTN_HEADS = 4
CONV_CTX_ROWS = 32
CONV_ROW_CHUNK = 16
VMEM_LIMIT_BYTES = 56 * 1024 * 1024


def _layer_norm(y, g, b):
    mu = jnp.mean(y, axis=-1, keepdims=True)
    d = y - mu
    var = jnp.mean(d * d, axis=-1, keepdims=True)
    return d * lax.rsqrt(var + LN_EPS) * g + b


def _dot(a, b):
    return jnp.dot(a, b, preferred_element_type=F32)


def _dot_nt(a, b):
    return lax.dot_general(a, b, (((1,), (1,)), ((), ())), preferred_element_type=F32)


def _params(*sem):
    return pltpu.CompilerParams(dimension_semantics=sem, vmem_limit_bytes=VMEM_LIMIT_BYTES)


def _ffn_kernel(x_ref, wg_ref, wu_ref, wd_ref, g_ref, b_ref, o_ref, xb_ref, acc_ref, *, alpha):
    f = pl.program_id(1)

    @pl.when(f == 0)
    def _():
        xb_ref[...] = x_ref[...].astype(BF16)
        acc_ref[...] = jnp.zeros_like(acc_ref)

    xb = xb_ref[...]
    gate = _dot(xb, wg_ref[...])
    up = _dot(xb, wu_ref[...])
    h = (gate * jax.nn.sigmoid(gate) * up).astype(BF16)
    acc_ref[...] += _dot(h, wd_ref[...])

    @pl.when(f == pl.num_programs(1) - 1)
    def _():
        y = alpha * x_ref[...] + 0.5 * acc_ref[...]
        o_ref[...] = _layer_norm(y, g_ref[...], b_ref[...])


def _ffn_post_norm(x, wg, wu, wd, g, b, *, alpha, tm, tf):
    n, d = x.shape
    dff = wg.shape[1]
    return pl.pallas_call(
        functools.partial(_ffn_kernel, alpha=alpha),
        out_shape=jax.ShapeDtypeStruct((n, d), F32),
        grid=(n // tm, dff // tf),
        in_specs=[
            pl.BlockSpec((tm, d), lambda i, f: (i, 0)),
            pl.BlockSpec((d, tf), lambda i, f: (0, f)),
            pl.BlockSpec((d, tf), lambda i, f: (0, f)),
            pl.BlockSpec((tf, d), lambda i, f: (f, 0)),
            pl.BlockSpec((1, d), lambda i, f: (0, 0)),
            pl.BlockSpec((1, d), lambda i, f: (0, 0)),
        ],
        out_specs=pl.BlockSpec((tm, d), lambda i, f: (i, 0)),
        scratch_shapes=[pltpu.VMEM((tm, d), BF16), pltpu.VMEM((tm, d), F32)],
        compiler_params=_params("parallel", "arbitrary"),
        name="ffn_post_norm",
    )(x, wg, wu, wd, g, b)


def _in_proj_kernel(x_ref, w_ref, h_ref, q_ref, k_ref, v_ref, xb_ref, a_ref):
    j = pl.program_id(1)

    @pl.when(j == 0)
    def _():
        xb_ref[...] = x_ref[...].astype(BF16)

    r = _dot(xb_ref[...], w_ref[...])

    @pl.when(j == 0)
    def _():
        a_ref[...] = r

    @pl.when(j == 1)
    def _():
        h_ref[...] = a_ref[...] * jax.nn.sigmoid(r)

    @pl.when(j == 2)
    def _():
        q_ref[...] = r.astype(BF16)

    @pl.when(j == 3)
    def _():
        k_ref[...] = r

    @pl.when(j == 4)
    def _():
        v_ref[...] = r


def _in_proj(x, w_in, *, tm):
    n, d = x.shape
    cols = w_in.shape[1]
    c = cols // 5
    out_block = pl.BlockSpec((tm, c), lambda i, j: (i, 0))
    return pl.pallas_call(
        _in_proj_kernel,
        out_shape=(
            jax.ShapeDtypeStruct((n, c), F32),
            jax.ShapeDtypeStruct((n, c), BF16),
            jax.ShapeDtypeStruct((n, c), F32),
            jax.ShapeDtypeStruct((n, c), F32),
        ),
        grid=(n // tm, 5),
        in_specs=[
            pl.BlockSpec((tm, d), lambda i, j: (i, 0)),
            pl.BlockSpec((d, c), lambda i, j: (0, j)),
        ],
        out_specs=(out_block, out_block, out_block, out_block),
        scratch_shapes=[pltpu.VMEM((tm, d), BF16), pltpu.VMEM((tm, c), F32)],
        compiler_params=_params("parallel", "arbitrary"),
        name="in_proj",
    )(x, w_in)


def _conv_kernel(h_ref, prev_ref, w_ref, cb_ref, g_ref, b_ref, y_ref, st_ref, hp_ref, *, tt):
    t = pl.program_id(1)
    ctx = CONV_CTX_ROWS
    pad = ctx - CONV_STATE

    @pl.when(t == 0)
    def _():
        hp_ref[0:pad, :] = jnp.zeros((pad, hp_ref.shape[1]), F32)
        hp_ref[pad:ctx, :] = prev_ref[0]

    @pl.when(t > 0)
    def _():
        hp_ref[0:ctx, :] = hp_ref[tt:tt + ctx, :]

    hp_ref[ctx:ctx + tt, :] = h_ref[0]

    rc = CONV_ROW_CHUNK
    for c in range(tt // rc):
        r0 = c * rc
        acc = jnp.broadcast_to(cb_ref[...], (rc, hp_ref.shape[1]))
        for j in range(CONV_WIDTH):
            acc = acc + w_ref[j:j + 1, :] * hp_ref[r0 + pad + j:r0 + pad + j + rc, :]
        y = _layer_norm(acc, g_ref[...], b_ref[...])
        y_ref[0, r0:r0 + rc, :] = (y * jax.nn.sigmoid(y)).astype(y_ref.dtype)

    @pl.when(t == pl.num_programs(1) - 1)
    def _():
        st_ref[0] = hp_ref[tt + pad:tt + ctx, :]


def _conv_module(h, prev, conv_w, conv_b, ln_g, ln_b, *, tt):
    bsz, t, c = h.shape
    vec = pl.BlockSpec((1, c), lambda b, i: (0, 0))
    return pl.pallas_call(
        functools.partial(_conv_kernel, tt=tt),
        out_shape=(
            jax.ShapeDtypeStruct((bsz, t, c), BF16),
            jax.ShapeDtypeStruct((bsz, CONV_STATE, c), F32),
        ),
        grid=(bsz, t // tt),
        in_specs=[
            pl.BlockSpec((1, tt, c), lambda b, i: (b, i, 0)),
            pl.BlockSpec((1, CONV_STATE, c), lambda b, i: (b, 0, 0)),
            pl.BlockSpec((CONV_WIDTH, c), lambda b, i: (0, 0)),
            vec, vec, vec,
        ],
        out_specs=(
            pl.BlockSpec((1, tt, c), lambda b, i: (b, i, 0)),
            pl.BlockSpec((1, CONV_STATE, c), lambda b, i: (b, 0, 0)),
        ),
        scratch_shapes=[pltpu.VMEM((tt + CONV_CTX_ROWS, c), F32)],
        compiler_params=_params("parallel", "arbitrary"),
        name="conv_module",
    )(h, prev, conv_w, conv_b, ln_g, ln_b)


def _sb_block(qb, kblk, vblk, tri, carry, acc, mask, scale):
    z = _dot_nt(qb, kblk) * scale
    sp = jnp.maximum(z, 0.0) + jnp.log1p(jnp.exp(-jnp.abs(z)))
    if mask is not None:
        sp = jnp.where(mask, sp, 0.0)
    hi = sp.astype(BF16)
    lo = (sp - hi.astype(F32)).astype(BF16)
    later = _dot(hi, tri) + _dot(lo, tri)
    w = jnp.exp(z - sp - later - carry)
    if mask is not None:
        w = jnp.where(mask, w, 0.0)
    acc = acc + _dot(w.astype(BF16), vblk)
    carry = carry + jnp.sum(sp, axis=1, keepdims=True)
    return carry, acc


def _causal_mask(n):
    row = lax.broadcasted_iota(jnp.int32, (n, n), 0)
    col = lax.broadcasted_iota(jnp.int32, (n, n), 1)
    return col < row


def _sb_prompt_kernel(q_ref, k_ref, v_ref, tri_ref, o_ref, kb_ref, vb_ref, *, tq, scale):
    qi = pl.program_id(2)

    @pl.when(qi == 0)
    def _():
        kb_ref[...] = k_ref[0].astype(BF16)
        vb_ref[...] = v_ref[0].astype(BF16)

    qb = q_ref[...]
    tri = tri_ref[...]
    d0 = pl.multiple_of(qi * tq, tq)
    carry = jnp.zeros((tq, 1), F32)
    acc = jnp.zeros((tq, qb.shape[1]), F32)
    carry, acc = _sb_block(qb, kb_ref[pl.ds(d0, tq), :], vb_ref[pl.ds(d0, tq), :], tri,
                           carry, acc, _causal_mask(tq), scale)

    def past(i, state):
        s0 = pl.multiple_of((qi - 1 - i) * tq, tq)
        return _sb_block(qb, kb_ref[pl.ds(s0, tq), :], vb_ref[pl.ds(s0, tq), :], tri,
                         state[0], state[1], None, scale)

    carry, acc = lax.fori_loop(0, qi, past, (carry, acc))
    o_ref[...] = acc.astype(o_ref.dtype)


def _sb_attention_prompt(q, k, v, tri, *, bsz, t, tq):
    n, c = q.shape
    dh = c // SB_HEADS
    nq = t // tq
    k3 = k.reshape(bsz, t, c)
    v3 = v.reshape(bsz, t, c)
    return pl.pallas_call(
        functools.partial(_sb_prompt_kernel, tq=tq, scale=dh ** -0.5),
        out_shape=jax.ShapeDtypeStruct((n, c), BF16),
        grid=(bsz, SB_HEADS, nq),
        in_specs=[
            pl.BlockSpec((tq, dh), lambda b, h, i: (b * nq + i, h)),
            pl.BlockSpec((1, t, dh), lambda b, h, i: (b, 0, h)),
            pl.BlockSpec((1, t, dh), lambda b, h, i: (b, 0, h)),
            pl.BlockSpec((tq, tq), lambda b, h, i: (0, 0)),
        ],
        out_specs=pl.BlockSpec((tq, dh), lambda b, h, i: (b * nq + i, h)),
        scratch_shapes=[pltpu.VMEM((t, dh), BF16), pltpu.VMEM((t, dh), BF16)],
        compiler_params=_params("parallel", "parallel", "arbitrary"),
        name="sb_attention_prompt",
    )(q, k3, v3, tri)


def _sb_sample_kernel(q_ref, kn_ref, vn_ref, kp_ref, vp_ref, trin_ref, trip_ref, o_ref, *, tk, scale):
    qb = q_ref[...]
    tq = qb.shape[0]
    carry = jnp.zeros((tq, 1), F32)
    acc = jnp.zeros((tq, qb.shape[1]), F32)
    carry, acc = _sb_block(qb, kn_ref[0].astype(BF16), vn_ref[0].astype(BF16), trin_ref[...],
                           carry, acc, _causal_mask(tq), scale)
    n_past = kp_ref.shape[1] // tk
    for i in range(n_past):
        s0 = (n_past - 1 - i) * tk
        carry, acc = _sb_block(qb, kp_ref[0, s0:s0 + tk, :].astype(BF16),
                               vp_ref[0, s0:s0 + tk, :].astype(BF16), trip_ref[...],
                               carry, acc, None, scale)
    o_ref[...] = acc.astype(o_ref.dtype)


def _sb_attention_sample(q, k_new, v_new, k_past, v_past, tri_new, tri_past, *, bsz, t, tk):
    n, c = q.shape
    dh = c // SB_HEADS
    p_len = k_past.shape[1]
    new_spec = pl.BlockSpec((1, t, dh), lambda b, h: (b, 0, h))
    past_spec = pl.BlockSpec((1, p_len, dh), lambda b, h: (b, 0, h))
    return pl.pallas_call(
        functools.partial(_sb_sample_kernel, tk=tk, scale=dh ** -0.5),
        out_shape=jax.ShapeDtypeStruct((n, c), BF16),
        grid=(bsz, SB_HEADS),
        in_specs=[
            pl.BlockSpec((t, dh), lambda b, h: (b, h)),
            new_spec, new_spec, past_spec, past_spec,
            pl.BlockSpec((t, t), lambda b, h: (0, 0)),
            pl.BlockSpec((tk, tk), lambda b, h: (0, 0)),
        ],
        out_specs=pl.BlockSpec((t, dh), lambda b, h: (b, h)),
        compiler_params=_params("parallel", "parallel"),
        name="sb_attention_sample",
    )(q, k_new.reshape(bsz, t, c), v_new.reshape(bsz, t, c), k_past, v_past, tri_new, tri_past)


def _strict_lower_ones(n):
    idx = jnp.arange(n)
    return (idx[:, None] > idx[None, :]).astype(BF16)


def _out_proj_kernel(x_ref, yc_ref, ys_ref, wc_ref, ws_ref, g_ref, b_ref, o_ref, *, alpha):
    mix = _dot(yc_ref[...], wc_ref[...]) + _dot(ys_ref[...], ws_ref[...])
    o_ref[...] = _layer_norm(alpha * x_ref[...] + mix, g_ref[...], b_ref[...])


def _out_proj_post_norm(x, y_conv, y_sb, w_out, g, b, *, alpha, tm):
    n, d = x.shape
    c = y_conv.shape[1]
    vec = pl.BlockSpec((1, d), lambda i: (0, 0))
    return pl.pallas_call(
        functools.partial(_out_proj_kernel, alpha=alpha),
        out_shape=jax.ShapeDtypeStruct((n, d), F32),
        grid=(n // tm,),
        in_specs=[
            pl.BlockSpec((tm, d), lambda i: (i, 0)),
            pl.BlockSpec((tm, c), lambda i: (i, 0)),
            pl.BlockSpec((tm, c), lambda i: (i, 0)),
            pl.BlockSpec((c, d), lambda i: (0, 0)),
            pl.BlockSpec((c, d), lambda i: (1, 0)),
            vec, vec,
        ],
        out_specs=pl.BlockSpec((tm, d), lambda i: (i, 0)),
        compiler_params=_params("parallel"),
        name="out_proj_post_norm",
    )(x, y_conv, y_sb, w_out, w_out, g, b)


def _mem_kv_kernel(m_ref, wk_ref, wv_ref, k_ref, v_ref):
    mb = m_ref[...].astype(BF16)
    k_ref[...] = _dot(mb, wk_ref[...])
    v_ref[...] = _dot(mb, wv_ref[...])


def _memory_kv(mem, wk, wv, *, tm):
    n, d = mem.shape
    c = wk.shape[1]
    w_spec = pl.BlockSpec((d, c), lambda i: (0, 0))
    o_spec = pl.BlockSpec((tm, c), lambda i: (i, 0))
    return pl.pallas_call(
        _mem_kv_kernel,
        out_shape=(jax.ShapeDtypeStruct((n, c), F32), jax.ShapeDtypeStruct((n, c), F32)),
        grid=(n // tm,),
        in_specs=[pl.BlockSpec((tm, d), lambda i: (i, 0)), w_spec, w_spec],
        out_specs=(o_spec, o_spec),
        compiler_params=_params("parallel"),
        name="memory_kv",
    )(mem, wk, wv)


def _xattn_kernel(x_ref, mk_ref, mv_ref, wq_ref, wo_ref, g_ref, b_ref, o_ref, *, alpha, scale):
    x = x_ref[...]
    q = _dot(x.astype(BF16), wq_ref[...])
    mk = mk_ref[0].astype(BF16)
    mv = mv_ref[0].astype(BF16)
    dh = q.shape[1] // XATTN_HEADS
    heads = []
    for h in range(XATTN_HEADS):
        sl = slice(h * dh, (h + 1) * dh)
        s = _dot_nt(q[:, sl].astype(BF16), mk[:, sl]) * scale
        e = jnp.exp(s - jnp.max(s, axis=-1, keepdims=True))
        p = e / jnp.sum(e, axis=-1, keepdims=True)
        heads.append(_dot(p.astype(BF16), mv[:, sl]).astype(BF16))
    o = jnp.concatenate(heads, axis=1)
    y = alpha * x + _dot(o, wo_ref[...])
    o_ref[...] = _layer_norm(y, g_ref[...], b_ref[...])


def _xattn_post_norm(x, mem_k, mem_v, wq, wo, g, b, *, alpha, bsz, t, tm):
    n, d = x.shape
    m, c = mem_k.shape[1], mem_k.shape[2]
    nt = t // tm
    vec = pl.BlockSpec((1, d), lambda bi, i: (0, 0))
    mem_spec = pl.BlockSpec((1, m, c), lambda bi, i: (bi, 0, 0))
    return pl.pallas_call(
        functools.partial(_xattn_kernel, alpha=alpha, scale=(c // XATTN_HEADS) ** -0.5),
        out_shape=jax.ShapeDtypeStruct((n, d), F32),
        grid=(bsz, nt),
        in_specs=[
            pl.BlockSpec((tm, d), lambda bi, i: (bi * nt + i, 0)),
            mem_spec, mem_spec,
            pl.BlockSpec((d, c), lambda bi, i: (0, 0)),
            pl.BlockSpec((c, d), lambda bi, i: (0, 0)),
            vec, vec,
        ],
        out_specs=pl.BlockSpec((tm, d), lambda bi, i: (bi * nt + i, 0)),
        compiler_params=_params("parallel", "parallel"),
        name="xattn_post_norm",
    )(x, mem_k, mem_v, wq, wo, g, b)


def _encoder_layer(x3, conv_prev, sb_past, mem_k, mem_v, w, *, alpha, tm, tt, tq):
    bsz, t, d = x3.shape
    x = x3.reshape(bsz * t, d)
    x = _ffn_post_norm(x, w["f1g"], w["f1u"], w["f1d"], w["ln1g"], w["ln1b"], alpha=alpha, tm=tm, tf=512)
    h, q, k, v = _in_proj(x, w["w_in"], tm=tm)
    c = h.shape[1]
    y_conv, conv_new = _conv_module(h.reshape(bsz, t, c), conv_prev, w["conv_w"], w["conv_b"],
                                    w["cln_g"], w["cln_b"], tt=tt)
    y_conv = y_conv.reshape(bsz * t, c)
    if sb_past is None:
        y_sb = _sb_attention_prompt(q, k, v, _strict_lower_ones(tq), bsz=bsz, t=t, tq=tq)
    else:
        k_past, v_past = sb_past
        y_sb = _sb_attention_sample(q, k, v, k_past, v_past, _strict_lower_ones(t), _strict_lower_ones(tq),
                                    bsz=bsz, t=t, tk=tq)
    x = _out_proj_post_norm(x, y_conv, y_sb, w["w_out"], w["ln2g"], w["ln2b"], alpha=alpha, tm=tm)
    x = _xattn_post_norm(x, mem_k, mem_v, w["xq"], w["xo"], w["ln3g"], w["ln3b"],
                         alpha=alpha, bsz=bsz, t=t, tm=min(tm, t))
    x = _ffn_post_norm(x, w["f2g"], w["f2u"], w["f2d"], w["ln4g"], w["ln4b"], alpha=alpha, tm=tm, tf=512)
    dh = c // SB_HEADS
    return (x.reshape(bsz, t, d), conv_new, k.reshape(bsz, t, SB_HEADS, dh), v.reshape(bsz, t, SB_HEADS, dh))


def kernel(x_prompt, x_sample, mem_prompt, cache_conv, cache_sb_k, cache_sb_v, cache_mem_k, cache_mem_v,
           ffn1_w_gate, ffn1_w_up, ffn1_w_down, ln1_g, ln1_b, w_in, conv_w, conv_b, conv_ln_g, conv_ln_b,
           w_out, ln2_g, ln2_b, xattn_wq, xattn_wk, xattn_wv, xattn_wo, ln3_g, ln3_b,
           ffn2_w_gate, ffn2_w_up, ffn2_w_down, ln4_g, ln4_b):
    depth = ffn1_w_gate.shape[0]
    alpha = (2.0 * depth) ** 0.25
    bp, tp, d = x_prompt.shape
    bs, ts, _ = x_sample.shape
    conv_dim = conv_w.shape[2]
    xattn_dim = xattn_wq.shape[2]

    xp, xs = x_prompt, x_sample
    conv_p, k_p, v_p, mk_p, mv_p, conv_s, k_s, v_s = [], [], [], [], [], [], [], []
    for l in range(depth):
        row = lambda a: a[l][None, :]
        w = dict(
            f1g=ffn1_w_gate[l].astype(BF16), f1u=ffn1_w_up[l].astype(BF16), f1d=ffn1_w_down[l].astype(BF16),
            ln1g=row(ln1_g), ln1b=row(ln1_b), w_in=w_in[l].astype(BF16),
            conv_w=conv_w[l], conv_b=row(conv_b), cln_g=row(conv_ln_g), cln_b=row(conv_ln_b),
            w_out=w_out[l].astype(BF16), ln2g=row(ln2_g), ln2b=row(ln2_b),
            xq=xattn_wq[l].astype(BF16), xo=xattn_wo[l].astype(BF16), ln3g=row(ln3_g), ln3b=row(ln3_b),
            f2g=ffn2_w_gate[l].astype(BF16), f2u=ffn2_w_up[l].astype(BF16), f2d=ffn2_w_down[l].astype(BF16),
            ln4g=row(ln4_g), ln4b=row(ln4_b),
        )
        m = mem_prompt.shape[1]
        mk, mv = _memory_kv(mem_prompt.reshape(bp * m, d), xattn_wk[l].astype(BF16), xattn_wv[l].astype(BF16),
                            tm=512)
        mk = mk.reshape(bp, m, xattn_dim)
        mv = mv.reshape(bp, m, xattn_dim)
        xp, cp, kp, vp = _encoder_layer(xp, jnp.zeros((bp, CONV_STATE, conv_dim), F32), None, mk, mv, w,
                                        alpha=alpha, tm=512, tt=256, tq=256)
        conv_p.append(cp); k_p.append(kp); v_p.append(vp)
        mk_p.append(mk.reshape(bp, m, XATTN_HEADS, xattn_dim // XATTN_HEADS))
        mv_p.append(mv.reshape(bp, m, XATTN_HEADS, xattn_dim // XATTN_HEADS))

        p_len = cache_sb_k.shape[2]
        sb_past = (cache_sb_k[l].reshape(bs, p_len, -1), cache_sb_v[l].reshape(bs, p_len, -1))
        xs, cs, ks, vs = _encoder_layer(xs, cache_conv[l], sb_past,
                                        cache_mem_k[l].reshape(bs, m, xattn_dim),
                                        cache_mem_v[l].reshape(bs, m, xattn_dim), w,
                                        alpha=alpha, tm=bs * ts, tt=ts, tq=256)
        conv_s.append(cs); k_s.append(ks); v_s.append(vs)
    return (xp, xs, jnp.stack(conv_p), jnp.stack(k_p), jnp.stack(v_p), jnp.stack(mk_p), jnp.stack(mv_p),
            jnp.stack(conv_s), jnp.stack(k_s), jnp.stack(v_s))
```

```python
import functools

import jax
import jax.numpy as jnp
from jax import lax
from jax.experimental import pallas as pl
from jax.experimental.pallas import tpu as pltpu

F32 = jnp.float32
BF16 = jnp.bfloat16

LN_EPS = 1e-5
CONV_WIDTH = 31
CONV_STATE = CONV_WIDTH - 1
SB_HEADS = 8
XATTN_HEADS = 4
CONV_CTX_ROWS = 32
CONV_ROW_CHUNK = 32
CONV_SUBLANES = 8
VMEM_LIMIT_BYTES = 56 * 1024 * 1024


def _layer_norm(y, g, b):
    mu = jnp.mean(y, axis=-1, keepdims=True)
    d = y - mu
    var = jnp.mean(d * d, axis=-1, keepdims=True)
    return d * lax.rsqrt(var + LN_EPS) * g + b


def _dot(a, b):
    return jnp.dot(a, b, preferred_element_type=F32)


def _dot_nt(a, b):
    return lax.dot_general(a, b, (((1,), (1,)), ((), ())), preferred_element_type=F32)


def _params(*sem):
    return pltpu.CompilerParams(dimension_semantics=sem, vmem_limit_bytes=VMEM_LIMIT_BYTES)


def _ffn_kernel(x_ref, wg_ref, wu_ref, wd_ref, g_ref, b_ref, o_ref, xb_ref, acc_ref, *, alpha):
    f = pl.program_id(1)

    @pl.when(f == 0)
    def _():
        xb_ref[...] = x_ref[...].astype(BF16)
        acc_ref[...] = jnp.zeros_like(acc_ref)

    xb = xb_ref[...]
    gate = _dot(xb, wg_ref[...])
    up = _dot(xb, wu_ref[...])
    h = (gate * jax.nn.sigmoid(gate) * up).astype(BF16)
    acc_ref[...] += _dot(h, wd_ref[...])

    @pl.when(f == pl.num_programs(1) - 1)
    def _():
        y = alpha * x_ref[...] + 0.5 * acc_ref[...]
        o_ref[...] = _layer_norm(y, g_ref[...], b_ref[...])


def _ffn_post_norm(x, wg, wu, wd, g, b, *, alpha, tm, tf):
    n, d = x.shape
    dff = wg.shape[1]
    return pl.pallas_call(
        functools.partial(_ffn_kernel, alpha=alpha),
        out_shape=jax.ShapeDtypeStruct((n, d), F32),
        grid=(n // tm, dff // tf),
        in_specs=[
            pl.BlockSpec((tm, d), lambda i, f: (i, 0)),
            pl.BlockSpec((d, tf), lambda i, f: (0, f)),
            pl.BlockSpec((d, tf), lambda i, f: (0, f)),
            pl.BlockSpec((tf, d), lambda i, f: (f, 0)),
            pl.BlockSpec((1, d), lambda i, f: (0, 0)),
            pl.BlockSpec((1, d), lambda i, f: (0, 0)),
        ],
        out_specs=pl.BlockSpec((tm, d), lambda i, f: (i, 0)),
        scratch_shapes=[pltpu.VMEM((tm, d), BF16), pltpu.VMEM((tm, d), F32)],
        compiler_params=_params("parallel", "arbitrary"),
        name="ffn_post_norm",
    )(x, wg, wu, wd, g, b)


def _store_heads_interleaved(dst_ref, r):
    n, c = r.shape
    dh = c // SB_HEADS
    for h in range(SB_HEADS):
        dst_ref[pl.ds(h, n, stride=SB_HEADS), :] = r[:, h * dh:(h + 1) * dh]


def _in_proj_kernel(x_ref, w_ref, h_ref, q_ref, k_ref, v_ref, kb_ref, vb_ref, xb_ref, a_ref):
    j = pl.program_id(1)

    @pl.when(j == 0)
    def _():
        xb_ref[...] = x_ref[...].astype(BF16)

    r = _dot(xb_ref[...], w_ref[...])

    @pl.when(j == 0)
    def _():
        a_ref[...] = r

    @pl.when(j == 1)
    def _():
        h_ref[...] = a_ref[...] * jax.nn.sigmoid(r)

    @pl.when(j == 2)
    def _():
        q_ref[...] = r.astype(BF16)

    @pl.when(j == 3)
    def _():
        _store_heads_interleaved(k_ref, r)
        kb_ref[...] = r.astype(BF16)

    @pl.when(j == 4)
    def _():
        _store_heads_interleaved(v_ref, r)
        vb_ref[...] = r.astype(BF16)


def _in_proj(x, w_in, *, tm):
    n, d = x.shape
    cols = w_in.shape[1]
    c = cols // 5
    out_block = pl.BlockSpec((tm, c), lambda i, j: (i, 0))
    dh = c // SB_HEADS
    state_block = pl.BlockSpec((tm * SB_HEADS, dh), lambda i, j: (i, 0))
    return pl.pallas_call(
        _in_proj_kernel,
        out_shape=(
            jax.ShapeDtypeStruct((n, c), F32),
            jax.ShapeDtypeStruct((n, c), BF16),
            jax.ShapeDtypeStruct((n * SB_HEADS, dh), F32),
            jax.ShapeDtypeStruct((n * SB_HEADS, dh), F32),
            jax.ShapeDtypeStruct((n, c), BF16),
            jax.ShapeDtypeStruct((n, c), BF16),
        ),
        grid=(n // tm, 5),
        in_specs=[
            pl.BlockSpec((tm, d), lambda i, j: (i, 0)),
            pl.BlockSpec((d, c), lambda i, j: (0, j)),
        ],
        out_specs=(out_block, out_block, state_block, state_block, out_block, out_block),
        scratch_shapes=[pltpu.VMEM((tm, d), BF16), pltpu.VMEM((tm, c), F32)],
        compiler_params=_params("parallel", "arbitrary"),
        name="in_proj",
    )(x, w_in)


def _conv_kernel(h_ref, prev_ref, w_ref, cb_ref, g_ref, b_ref, y_ref, st_ref, hp_ref, sh_ref, *, tt):
    t = pl.program_id(1)
    ctx = CONV_CTX_ROWS
    pad = ctx - CONV_STATE
    sub = CONV_SUBLANES

    @pl.when(t == 0)
    def _():
        hp_ref[0:pad, :] = jnp.zeros((pad, hp_ref.shape[1]), F32)
        hp_ref[pad:ctx, :] = prev_ref[0]

    @pl.when(t > 0)
    def _():
        hp_ref[0:ctx, :] = hp_ref[tt:tt + ctx, :]

    hp_ref[ctx:ctx + tt, :] = h_ref[0]

    span = tt + ctx - sub
    for r in range(1, sub):
        sh_ref[r - 1, 0:span, :] = hp_ref[r:r + span, :]

    rc = CONV_ROW_CHUNK

    def chunk(ci, carry):
        r0 = pl.multiple_of(ci * rc, rc)
        ch = hp_ref.shape[1]
        groups = rc // sub
        acc = [jnp.broadcast_to(cb_ref[...], (sub, ch)) for _ in range(groups)]
        for j in range(CONV_WIDTH):
            a, r = divmod(pad + j, sub)
            wj = w_ref[j]
            for gi in range(groups):
                lo = pl.multiple_of(r0 + (a + gi) * sub, sub)
                rows = hp_ref[pl.ds(lo, sub), :] if r == 0 else sh_ref[r - 1, pl.ds(lo, sub), :]
                acc[gi] = acc[gi] + wj * rows
        y = _layer_norm(jnp.concatenate(acc, axis=0), g_ref[...], b_ref[...])
        y_ref[0, pl.ds(r0, rc), :] = (y * jax.nn.sigmoid(y)).astype(y_ref.dtype)
        return carry

    lax.fori_loop(0, tt // rc, chunk, 0)

    @pl.when(t == pl.num_programs(1) - 1)
    def _():
        st_ref[0] = hp_ref[tt + pad:tt + ctx, :]


def _conv_module(h, prev, conv_w, conv_b, ln_g, ln_b, *, tt):
    bsz, t, c = h.shape
    vec = pl.BlockSpec((1, c), lambda b, i: (0, 0))
    return pl.pallas_call(
        functools.partial(_conv_kernel, tt=tt),
        out_shape=(
            jax.ShapeDtypeStruct((bsz, t, c), BF16),
            jax.ShapeDtypeStruct((bsz, CONV_STATE, c), F32),
        ),
        grid=(bsz, t // tt),
        in_specs=[
            pl.BlockSpec((1, tt, c), lambda b, i: (b, i, 0)),
            pl.BlockSpec((1, CONV_STATE, c), lambda b, i: (b, 0, 0)),
            pl.BlockSpec((CONV_WIDTH, CONV_SUBLANES, c), lambda b, i: (0, 0, 0)),
            vec, vec, vec,
        ],
        out_specs=(
            pl.BlockSpec((1, tt, c), lambda b, i: (b, i, 0)),
            pl.BlockSpec((1, CONV_STATE, c), lambda b, i: (b, 0, 0)),
        ),
        scratch_shapes=[pltpu.VMEM((tt + CONV_CTX_ROWS, c), F32),
                        pltpu.VMEM((CONV_SUBLANES - 1, tt + CONV_CTX_ROWS - CONV_SUBLANES, c), F32)],
        compiler_params=_params("parallel", "arbitrary"),
        name="conv_module",
    )(h, prev, jnp.broadcast_to(conv_w[:, None, :], (CONV_WIDTH, CONV_SUBLANES, c)), conv_b, ln_g, ln_b)


def _sb_blocks(qs, ks, vs, tri, state, mask, scale):
    n = len(qs)
    tq = qs[0].shape[0]
    zs = [_dot_nt(qs[a], ks[a]) * scale for a in range(n)]
    sps = []
    for z in zs:
        sp = jnp.maximum(z, 0.0) + jnp.log(1.0 + jnp.exp(-jnp.abs(z)))
        sps.append(sp if mask is None else jnp.where(mask, sp, 0.0))
    parts = []
    for sp in sps:
        hi = sp.astype(BF16)
        parts += [hi, (sp - hi.astype(F32)).astype(BF16)]
    sums = _dot(jnp.concatenate(parts, axis=0), tri)
    out = []
    for a in range(n):
        carry, acc = state[a]
        later = sums[2 * a * tq:(2 * a + 1) * tq] + sums[(2 * a + 1) * tq:(2 * a + 2) * tq]
        w = jnp.exp(zs[a] - sps[a] - later - carry)
        if mask is not None:
            w = jnp.where(mask, w, 0.0)
        out.append((carry + jnp.sum(sps[a], axis=1, keepdims=True), acc + _dot(w.astype(BF16), vs[a])))
    return tuple(out)


def _causal_mask(n):
    row = lax.broadcasted_iota(jnp.int32, (n, n), 0)
    col = lax.broadcasted_iota(jnp.int32, (n, n), 1)
    return col < row


def _sb_prompt_kernel(q_ref, k_ref, v_ref, tri_ref, o_ref, *, tq, dh, scale):
    qi = pl.program_id(2)
    heads = q_ref.shape[1] // dh
    tri = tri_ref[...]
    cols = [slice(a * dh, (a + 1) * dh) for a in range(heads)]
    qs = [q_ref[:, c] for c in cols]

    def visit(s0, state, mask):
        ks = [k_ref[0, pl.ds(s0, tq), c] for c in cols]
        vs = [v_ref[0, pl.ds(s0, tq), c] for c in cols]
        return _sb_blocks(qs, ks, vs, tri, state, mask, scale)

    state = tuple((jnp.zeros((tq, 1), F32), jnp.zeros((tq, dh), F32)) for _ in range(heads))
    state = visit(pl.multiple_of(qi * tq, tq), state, _causal_mask(tq))
    state = lax.fori_loop(0, qi, lambda i, st: visit(pl.multiple_of((qi - 1 - i) * tq, tq), st, None), state)
    for a in range(heads):
        o_ref[:, cols[a]] = state[a][1].astype(o_ref.dtype)


def _sb_attention_prompt(q, kb, vb, tri, *, bsz, t, tq, heads_per_step):
    n, c = q.shape
    dh = c // SB_HEADS
    nq = t // tq
    w = heads_per_step * dh
    kv_spec = pl.BlockSpec((1, t, w), lambda b, g, i: (b, 0, g))
    return pl.pallas_call(
        functools.partial(_sb_prompt_kernel, tq=tq, dh=dh, scale=dh ** -0.5),
        out_shape=jax.ShapeDtypeStruct((n, c), BF16),
        grid=(bsz, SB_HEADS // heads_per_step, nq),
        in_specs=[
            pl.BlockSpec((tq, w), lambda b, g, i: (b * nq + i, g)),
            kv_spec, kv_spec,
            pl.BlockSpec((tq, tq), lambda b, g, i: (0, 0)),
        ],
        out_specs=pl.BlockSpec((tq, w), lambda b, g, i: (b * nq + i, g)),
        compiler_params=_params("parallel", "parallel", "arbitrary"),
        name="sb_attention_prompt",
    )(q, kb.reshape(bsz, t, c), vb.reshape(bsz, t, c), tri)


def _sb_sample_kernel(q_ref, kn_ref, vn_ref, kp_ref, vp_ref, trin_ref, trip_ref, o_ref, *, tk, dh, scale):
    tq = q_ref.shape[0]
    cols = [slice(a * dh, (a + 1) * dh) for a in range(SB_HEADS)]
    qs = [q_ref[:, c] for c in cols]
    tri_new = trin_ref[...]
    tri_past = trip_ref[...]
    state = tuple((jnp.zeros((tq, 1), F32), jnp.zeros((tq, dh), F32)) for _ in range(SB_HEADS))
    state = _sb_blocks(qs, [kn_ref[:, c] for c in cols], [vn_ref[:, c] for c in cols], tri_new,
                       state, _causal_mask(tq), scale)
    n_past = kp_ref.shape[1] // (tk * SB_HEADS)
    for i in range(n_past):
        s0 = (n_past - 1 - i) * tk * SB_HEADS
        rows = [pl.ds(s0 + a, tk, stride=SB_HEADS) for a in range(SB_HEADS)]
        state = _sb_blocks(qs, [kp_ref[0, r, :].astype(BF16) for r in rows],
                           [vp_ref[0, r, :].astype(BF16) for r in rows], tri_past, state, None, scale)
    for a in range(SB_HEADS):
        o_ref[:, cols[a]] = state[a][1].astype(o_ref.dtype)


def _sb_attention_sample(q, kb_new, vb_new, k_past, v_past, tri_new, tri_past, *, bsz, t, tk):
    n, c = q.shape
    dh = c // SB_HEADS
    rows_past = k_past.shape[1]
    new_spec = pl.BlockSpec((t, c), lambda b: (b, 0))
    past_spec = pl.BlockSpec((1, rows_past, dh), lambda b: (b, 0, 0))
    return pl.pallas_call(
        functools.partial(_sb_sample_kernel, tk=tk, dh=dh, scale=dh ** -0.5),
        out_shape=jax.ShapeDtypeStruct((n, c), BF16),
        grid=(bsz,),
        in_specs=[
            new_spec, new_spec, new_spec, past_spec, past_spec,
            pl.BlockSpec((t, t), lambda b: (0, 0)),
            pl.BlockSpec((tk, tk), lambda b: (0, 0)),
        ],
        out_specs=new_spec,
        compiler_params=_params("parallel"),
        name="sb_attention_sample",
    )(q, kb_new, vb_new, k_past, v_past, tri_new, tri_past)


def _strict_lower_ones(n):
    idx = jnp.arange(n)
    return (idx[:, None] > idx[None, :]).astype(BF16)


def _out_proj_kernel(x_ref, yc_ref, ys_ref, wc_ref, ws_ref, g_ref, b_ref, o_ref, *, alpha):
    mix = _dot(yc_ref[...], wc_ref[...]) + _dot(ys_ref[...], ws_ref[...])
    o_ref[...] = _layer_norm(alpha * x_ref[...] + mix, g_ref[...], b_ref[...])


def _out_proj_post_norm(x, y_conv, y_sb, w_out, g, b, *, alpha, tm):
    n, d = x.shape
    c = y_conv.shape[1]
    vec = pl.BlockSpec((1, d), lambda i: (0, 0))
    return pl.pallas_call(
        functools.partial(_out_proj_kernel, alpha=alpha),
        out_shape=jax.ShapeDtypeStruct((n, d), F32),
        grid=(n // tm,),
        in_specs=[
            pl.BlockSpec((tm, d), lambda i: (i, 0)),
            pl.BlockSpec((tm, c), lambda i: (i, 0)),
            pl.BlockSpec((tm, c), lambda i: (i, 0)),
            pl.BlockSpec((c, d), lambda i: (0, 0)),
            pl.BlockSpec((c, d), lambda i: (1, 0)),
            vec, vec,
        ],
        out_specs=pl.BlockSpec((tm, d), lambda i: (i, 0)),
        compiler_params=_params("parallel"),
        name="out_proj_post_norm",
    )(x, y_conv, y_sb, w_out, w_out, g, b)


def _mem_kv_kernel(m_ref, wk_ref, wv_ref, k_ref, v_ref):
    mb = m_ref[...].astype(BF16)
    k_ref[...] = _dot(mb, wk_ref[...])
    v_ref[...] = _dot(mb, wv_ref[...])


def _memory_kv(mem, wk, wv, *, tm):
    n, d = mem.shape
    c = wk.shape[1]
    w_spec = pl.BlockSpec((d, c), lambda i: (0, 0))
    o_spec = pl.BlockSpec((tm, c), lambda i: (i, 0))
    return pl.pallas_call(
        _mem_kv_kernel,
        out_shape=(jax.ShapeDtypeStruct((n, c), F32), jax.ShapeDtypeStruct((n, c), F32)),
        grid=(n // tm,),
        in_specs=[pl.BlockSpec((tm, d), lambda i: (i, 0)), w_spec, w_spec],
        out_specs=(o_spec, o_spec),
        compiler_params=_params("parallel"),
        name="memory_kv",
    )(mem, wk, wv)


def _xattn_kernel(x_ref, mk_ref, mv_ref, wq_ref, wo_ref, g_ref, b_ref, o_ref, *, alpha, scale):
    x = x_ref[...]
    q = _dot(x.astype(BF16), wq_ref[...])
    mk = mk_ref[0].astype(BF16)
    mv = mv_ref[0].astype(BF16)
    dh = q.shape[1] // XATTN_HEADS
    heads = []
    for h in range(XATTN_HEADS):
        sl = slice(h * dh, (h + 1) * dh)
        s = _dot_nt(q[:, sl].astype(BF16), mk[:, sl]) * scale
        e = jnp.exp(s - jnp.max(s, axis=-1, keepdims=True))
        p = e / jnp.sum(e, axis=-1, keepdims=True)
        heads.append(_dot(p.astype(BF16), mv[:, sl]).astype(BF16))
    o = jnp.concatenate(heads, axis=1)
    y = alpha * x + _dot(o, wo_ref[...])
    o_ref[...] = _layer_norm(y, g_ref[...], b_ref[...])


def _xattn_post_norm(x, mem_k, mem_v, wq, wo, g, b, *, alpha, bsz, t, tm):
    n, d = x.shape
    m, c = mem_k.shape[1], mem_k.shape[2]
    nt = t // tm
    vec = pl.BlockSpec((1, d), lambda bi, i: (0, 0))
    mem_spec = pl.BlockSpec((1, m, c), lambda bi, i: (bi, 0, 0))
    return pl.pallas_call(
        functools.partial(_xattn_kernel, alpha=alpha, scale=(c // XATTN_HEADS) ** -0.5),
        out_shape=jax.ShapeDtypeStruct((n, d), F32),
        grid=(bsz, nt),
        in_specs=[
            pl.BlockSpec((tm, d), lambda bi, i: (bi * nt + i, 0)),
            mem_spec, mem_spec,
            pl.BlockSpec((d, c), lambda bi, i: (0, 0)),
            pl.BlockSpec((c, d), lambda bi, i: (0, 0)),
            vec, vec,
        ],
        out_specs=pl.BlockSpec((tm, d), lambda bi, i: (bi * nt + i, 0)),
        compiler_params=_params("parallel", "parallel"),
        name="xattn_post_norm",
    )(x, mem_k, mem_v, wq, wo, g, b)


def _encoder_layer(x3, conv_prev, sb_past, mem_k, mem_v, w, *, alpha, tm, tt, tq):
    bsz, t, d = x3.shape
    x = x3.reshape(bsz * t, d)
    x = _ffn_post_norm(x, w["f1g"], w["f1u"], w["f1d"], w["ln1g"], w["ln1b"], alpha=alpha, tm=tm, tf=512)
    h, q, k, v, kb, vb = _in_proj(x, w["w_in"], tm=tm)
    c = h.shape[1]
    y_conv, conv_new = _conv_module(h.reshape(bsz, t, c), conv_prev, w["conv_w"], w["conv_b"],
                                    w["cln_g"], w["cln_b"], tt=tt)
    y_conv = y_conv.reshape(bsz * t, c)
    if sb_past is None:
        y_sb = _sb_attention_prompt(q, kb, vb, _strict_lower_ones(tq), bsz=bsz, t=t, tq=tq, heads_per_step=4)
    else:
        k_past, v_past = sb_past
        y_sb = _sb_attention_sample(q, kb, vb, k_past, v_past, _strict_lower_ones(t), _strict_lower_ones(tq),
                                    bsz=bsz, t=t, tk=tq)
    x = _out_proj_post_norm(x, y_conv, y_sb, w["w_out"], w["ln2g"], w["ln2b"], alpha=alpha, tm=tm)
    x = _xattn_post_norm(x, mem_k, mem_v, w["xq"], w["xo"], w["ln3g"], w["ln3b"],
                         alpha=alpha, bsz=bsz, t=t, tm=min(tm, t))
    x = _ffn_post_norm(x, w["f2g"], w["f2u"], w["f2d"], w["ln4g"], w["ln4b"], alpha=alpha, tm=tm, tf=512)
    dh = c // SB_HEADS
    return (x.reshape(bsz, t, d), conv_new, k.reshape(bsz, t, SB_HEADS, dh), v.reshape(bsz, t, SB_HEADS, dh))


def kernel(x_prompt, x_sample, mem_prompt, cache_conv, cache_sb_k, cache_sb_v, cache_mem_k, cache_mem_v,
           ffn1_w_gate, ffn1_w_up, ffn1_w_down, ln1_g, ln1_b, w_in, conv_w, conv_b, conv_ln_g, conv_ln_b,
           w_out, ln2_g, ln2_b, xattn_wq, xattn_wk, xattn_wv, xattn_wo, ln3_g, ln3_b,
           ffn2_w_gate, ffn2_w_up, ffn2_w_down, ln4_g, ln4_b):
    depth = ffn1_w_gate.shape[0]
    alpha = (2.0 * depth) ** 0.25
    bp, tp, d = x_prompt.shape
    bs, ts, _ = x_sample.shape
    conv_dim = conv_w.shape[2]
    xattn_dim = xattn_wq.shape[2]

    xp, xs = x_prompt, x_sample
    conv_p, k_p, v_p, mk_p, mv_p, conv_s, k_s, v_s = [], [], [], [], [], [], [], []
    for l in range(depth):
        row = lambda a: a[l][None, :]
        w = dict(
            f1g=ffn1_w_gate[l].astype(BF16), f1u=ffn1_w_up[l].astype(BF16), f1d=ffn1_w_down[l].astype(BF16),
            ln1g=row(ln1_g), ln1b=row(ln1_b), w_in=w_in[l].astype(BF16),
            conv_w=conv_w[l], conv_b=row(conv_b), cln_g=row(conv_ln_g), cln_b=row(conv_ln_b),
            w_out=w_out[l].astype(BF16), ln2g=row(ln2_g), ln2b=row(ln2_b),
            xq=xattn_wq[l].astype(BF16), xo=xattn_wo[l].astype(BF16), ln3g=row(ln3_g), ln3b=row(ln3_b),
            f2g=ffn2_w_gate[l].astype(BF16), f2u=ffn2_w_up[l].astype(BF16), f2d=ffn2_w_down[l].astype(BF16),
            ln4g=row(ln4_g), ln4b=row(ln4_b),
        )
        m = mem_prompt.shape[1]
        mk, mv = _memory_kv(mem_prompt.reshape(bp * m, d), xattn_wk[l].astype(BF16), xattn_wv[l].astype(BF16),
                            tm=512)
        mk = mk.reshape(bp, m, xattn_dim)
        mv = mv.reshape(bp, m, xattn_dim)
        xp, cp, kp, vp = _encoder_layer(xp, jnp.zeros((bp, CONV_STATE, conv_dim), F32), None, mk, mv, w,
                                        alpha=alpha, tm=512, tt=256, tq=256)
        conv_p.append(cp); k_p.append(kp); v_p.append(vp)
        mk_p.append(mk.reshape(bp, m, XATTN_HEADS, xattn_dim // XATTN_HEADS))
        mv_p.append(mv.reshape(bp, m, XATTN_HEADS, xattn_dim // XATTN_HEADS))

        p_len = cache_sb_k.shape[2]
        dh = cache_sb_k.shape[4]
        sb_past = (cache_sb_k[l].reshape(bs, p_len * SB_HEADS, dh), cache_sb_v[l].reshape(bs, p_len * SB_HEADS, dh))
        xs, cs, ks, vs = _encoder_layer(xs, cache_conv[l], sb_past,
                                        cache_mem_k[l].reshape(bs, m, xattn_dim),
                                        cache_mem_v[l].reshape(bs, m, xattn_dim), w,
                                        alpha=alpha, tm=bs * ts, tt=ts, tq=256)
        conv_s.append(cs); k_s.append(ks); v_s.append(vs)
    stack = lambda parts: parts[0][None] if len(parts) == 1 else jnp.stack(parts)
    return (xp, xs, stack(conv_p), stack(k_p), stack(v_p), stack(mk_p), stack(mv_p),
            stack(conv_s), stack(k_s), stack(v_s))
```

```python
import functools

import jax
import jax.numpy as jnp
from jax import lax
from jax.experimental import pallas as pl
from jax.experimental.pallas import tpu as pltpu

F32 = jnp.float32
BF16 = jnp.bfloat16

LN_EPS = 1e-5
CONV_WIDTH = 31
CONV_STATE = CONV_WIDTH - 1
SB_HEADS = 8
XATTN_HEADS = 4
CONV_CTX_ROWS = 32
CONV_ROW_CHUNK = 32
CONV_SUBLANES = 8
MXU_ROWS = 256
IN_PROJ_ROWS = 256
SB_CARRY_CUTOFF = 110.0
VMEM_LIMIT_BYTES = 56 * 1024 * 1024


def _layer_norm(y, g, b):
    mu = jnp.mean(y, axis=-1, keepdims=True)
    d = y - mu
    var = jnp.mean(d * d, axis=-1, keepdims=True)
    return d * lax.rsqrt(var + LN_EPS) * g + b


def _dot(a, b):
    return jnp.dot(a, b, preferred_element_type=F32)


def _dot_nt(a, b):
    return lax.dot_general(a, b, (((1,), (1,)), ((), ())), preferred_element_type=F32)


def _row_parts(tm):
    parts = 2 if tm % (2 * MXU_ROWS) == 0 else 1
    return [slice(r * tm // parts, (r + 1) * tm // parts) for r in range(parts)]


def _params(*sem):
    return pltpu.CompilerParams(dimension_semantics=sem, vmem_limit_bytes=VMEM_LIMIT_BYTES)


def _ffn_kernel(x_ref, wg_ref, wu_ref, wd_ref, g_ref, b_ref, o_ref, acc_ref, *, alpha, nf):
    f = pl.program_id(1)

    def step(first, last):
        tm = x_ref.shape[0]
        parts = 2 if last else 1
        for r in range(parts):
            rows = slice(r * tm // parts, (r + 1) * tm // parts)
            x = x_ref[rows, :]
            xb = x.astype(BF16)
            gate = _dot(xb, wg_ref[...])
            up = _dot(xb, wu_ref[...])
            h = (gate * jax.nn.sigmoid(gate) * up).astype(BF16)
            total = _dot(h, wd_ref[...])
            if not first:
                total = acc_ref[rows, :] + total
            if last:
                o_ref[rows, :] = _layer_norm(alpha * x + 0.5 * total, g_ref[...], b_ref[...])
            else:
                acc_ref[rows, :] = total

    if nf == 1:
        step(True, True)
    else:
        pl.when(f == 0)(lambda: step(True, False))
        pl.when(jnp.logical_and(f > 0, f < nf - 1))(lambda: step(False, False))
        pl.when(f == nf - 1)(lambda: step(False, True))


def _ffn_post_norm(x, wg, wu, wd, g, b, *, alpha, tm, tf):
    n, d = x.shape
    dff = wg.shape[1]
    return pl.pallas_call(
        functools.partial(_ffn_kernel, alpha=alpha, nf=dff // tf),
        out_shape=jax.ShapeDtypeStruct((n, d), F32),
        grid=(n // tm, dff // tf),
        in_specs=[
            pl.BlockSpec((tm, d), lambda i, f: (i, 0)),
            pl.BlockSpec((d, tf), lambda i, f: (0, f)),
            pl.BlockSpec((d, tf), lambda i, f: (0, f)),
            pl.BlockSpec((tf, d), lambda i, f: (f, 0)),
            pl.BlockSpec((1, d), lambda i, f: (0, 0)),
            pl.BlockSpec((1, d), lambda i, f: (0, 0)),
        ],
        out_specs=pl.BlockSpec((tm, d), lambda i, f: (i, 0)),
        scratch_shapes=[pltpu.VMEM((tm, d), F32)],
        compiler_params=_params("parallel", "arbitrary"),
        name="ffn_post_norm",
    )(x, wg, wu, wd, g, b)


def _store_heads_interleaved(dst_ref, r):
    n, c = r.shape
    dh = c // SB_HEADS
    for h in range(SB_HEADS):
        dst_ref[pl.ds(h, n, stride=SB_HEADS), :] = r[:, h * dh:(h + 1) * dh]


def _in_proj_kernel(x_ref, w_ref, h_ref, q_ref, k_ref, v_ref, kb_ref, vb_ref):
    xb = x_ref[...].astype(BF16)
    c = h_ref.shape[1]
    group = lambda j: _dot(xb, w_ref[:, j * c:(j + 1) * c])
    h_ref[...] = group(0) * jax.nn.sigmoid(group(1))
    q_ref[...] = group(2).astype(BF16)
    k = group(3)
    _store_heads_interleaved(k_ref, k)
    kb_ref[...] = k.astype(BF16)
    v = group(4)
    _store_heads_interleaved(v_ref, v)
    vb_ref[...] = v.astype(BF16)


def _in_proj(x, w_in, *, tm):
    n, d = x.shape
    cols = w_in.shape[1]
    c = cols // 5
    out_block = pl.BlockSpec((tm, c), lambda i: (i, 0))
    dh = c // SB_HEADS
    state_block = pl.BlockSpec((tm * SB_HEADS, dh), lambda i: (i, 0))
    return pl.pallas_call(
        _in_proj_kernel,
        out_shape=(
            jax.ShapeDtypeStruct((n, c), F32),
            jax.ShapeDtypeStruct((n, c), BF16),
            jax.ShapeDtypeStruct((n * SB_HEADS, dh), F32),
            jax.ShapeDtypeStruct((n * SB_HEADS, dh), F32),
            jax.ShapeDtypeStruct((n, c), BF16),
            jax.ShapeDtypeStruct((n, c), BF16),
        ),
        grid=(n // tm,),
        in_specs=[
            pl.BlockSpec((tm, d), lambda i: (i, 0)),
            pl.BlockSpec((d, cols), lambda i: (0, 0), pipeline_mode=pl.Buffered(1)),
        ],
        out_specs=(out_block, out_block, state_block, state_block, out_block, out_block),
        compiler_params=_params("parallel"),
        name="in_proj",
    )(x, w_in)


def _conv_kernel(h_ref, prev_ref, w_ref, cb_ref, g_ref, b_ref, y_ref, st_ref, hp_ref, sh_ref, *, tt):
    t = pl.program_id(1)
    ctx = CONV_CTX_ROWS
    pad = ctx - CONV_STATE
    sub = CONV_SUBLANES

    @pl.when(t == 0)
    def _():
        hp_ref[0:pad, :] = jnp.zeros((pad, hp_ref.shape[1]), F32)
        hp_ref[pad:ctx, :] = prev_ref[0]

    @pl.when(t > 0)
    def _():
        hp_ref[0:ctx, :] = hp_ref[tt:tt + ctx, :]

    hp_ref[ctx:ctx + tt, :] = h_ref[0]

    span = tt + ctx - sub
    for r in range(1, sub):
        sh_ref[r - 1, 0:span, :] = hp_ref[r:r + span, :]

    rc = CONV_ROW_CHUNK

    def chunk(ci, carry):
        r0 = pl.multiple_of(ci * rc, rc)
        ch = hp_ref.shape[1]
        groups = rc // sub
        acc = [jnp.broadcast_to(cb_ref[...], (sub, ch)) for _ in range(groups)]
        for j in range(CONV_WIDTH):
            a, r = divmod(pad + j, sub)
            wj = w_ref[j]
            for gi in range(groups):
                lo = pl.multiple_of(r0 + (a + gi) * sub, sub)
                rows = hp_ref[pl.ds(lo, sub), :] if r == 0 else sh_ref[r - 1, pl.ds(lo, sub), :]
                acc[gi] = acc[gi] + wj * rows
        y = _layer_norm(jnp.concatenate(acc, axis=0), g_ref[...], b_ref[...])
        y_ref[0, pl.ds(r0, rc), :] = (y * jax.nn.sigmoid(y)).astype(y_ref.dtype)
        return carry

    lax.fori_loop(0, tt // rc, chunk, 0)

    @pl.when(t == pl.num_programs(1) - 1)
    def _():
        st_ref[0] = hp_ref[tt + pad:tt + ctx, :]


def _conv_module(h, prev, conv_w, conv_b, ln_g, ln_b, *, tt):
    bsz, t, c = h.shape
    vec = pl.BlockSpec((1, c), lambda b, i: (0, 0))
    return pl.pallas_call(
        functools.partial(_conv_kernel, tt=tt),
        out_shape=(
            jax.ShapeDtypeStruct((bsz, t, c), BF16),
            jax.ShapeDtypeStruct((bsz, CONV_STATE, c), F32),
        ),
        grid=(bsz, t // tt),
        in_specs=[
            pl.BlockSpec((1, tt, c), lambda b, i: (b, i, 0)),
            pl.BlockSpec((1, CONV_STATE, c), lambda b, i: (b, 0, 0)),
            pl.BlockSpec((CONV_WIDTH, CONV_SUBLANES, c), lambda b, i: (0, 0, 0)),
            vec, vec, vec,
        ],
        out_specs=(
            pl.BlockSpec((1, tt, c), lambda b, i: (b, i, 0)),
            pl.BlockSpec((1, CONV_STATE, c), lambda b, i: (b, 0, 0)),
        ),
        scratch_shapes=[pltpu.VMEM((tt + CONV_CTX_ROWS, c), F32),
                        pltpu.VMEM((CONV_SUBLANES - 1, tt + CONV_CTX_ROWS - CONV_SUBLANES, c), F32)],
        compiler_params=_params("parallel", "arbitrary"),
        name="conv_module",
    )(h, prev, jnp.broadcast_to(conv_w[:, None, :], (CONV_WIDTH, CONV_SUBLANES, c)), conv_b, ln_g, ln_b)


def _sb_blocks(qs, ks, vs, tri, state, mask, scale):
    n = len(qs)
    tq = qs[0].shape[0]
    zs = [_dot_nt(qs[a], ks[a]) * scale for a in range(n)]
    sps = []
    for z in zs:
        sp = jnp.maximum(z, 0.0) + jnp.log(1.0 + jnp.exp(-jnp.abs(z)))
        sps.append(sp if mask is None else jnp.where(mask, sp, 0.0))
    parts = []
    for sp in sps:
        hi = sp.astype(BF16)
        parts += [hi, (sp - hi.astype(F32)).astype(BF16)]
    sums = _dot(jnp.concatenate(parts, axis=0), tri)
    out = []
    for a in range(n):
        carry, acc = state[a]
        later = sums[2 * a * tq:(2 * a + 1) * tq] + sums[(2 * a + 1) * tq:(2 * a + 2) * tq]
        w = jnp.exp(zs[a] - sps[a] - later - carry)
        if mask is not None:
            w = jnp.where(mask, w, 0.0)
        out.append((carry + jnp.sum(sps[a], axis=1, keepdims=True), acc + _dot(w.astype(BF16), vs[a])))
    return tuple(out)


def _causal_mask(n):
    row = lax.broadcasted_iota(jnp.int32, (n, n), 0)
    col = lax.broadcasted_iota(jnp.int32, (n, n), 1)
    return col < row


def _sb_prompt_kernel(q_ref, k_ref, v_ref, tri_ref, o_ref, *, tq, dh, scale):
    qi = pl.program_id(2)
    heads = q_ref.shape[1] // dh
    tri = tri_ref[...]
    cols = [slice(a * dh, (a + 1) * dh) for a in range(heads)]
    qs = [q_ref[:, c] for c in cols]

    def visit(s0, state, mask):
        ks = [k_ref[0, pl.ds(s0, tq), c] for c in cols]
        vs = [v_ref[0, pl.ds(s0, tq), c] for c in cols]
        return _sb_blocks(qs, ks, vs, tri, state, mask, scale)

    state = tuple((jnp.zeros((tq, 1), F32), jnp.zeros((tq, dh), F32)) for _ in range(heads))
    state = visit(pl.multiple_of(qi * tq, tq), state, _causal_mask(tq))

    def live(st):
        lowest = functools.reduce(jnp.minimum, [jnp.min(c) for c, _ in st])
        return lowest < SB_CARRY_CUTOFF

    def cond(loop):
        i, more, _ = loop
        return jnp.logical_and(i < qi, more)

    def body(loop):
        i, _, st = loop
        st = visit(pl.multiple_of((qi - 1 - i) * tq, tq), st, None)
        return i + 1, live(st), st

    state = lax.while_loop(cond, body, (jnp.int32(0), live(state), state))[2]
    for a in range(heads):
        o_ref[:, cols[a]] = state[a][1].astype(o_ref.dtype)


def _sb_attention_prompt(q, kb, vb, tri, *, bsz, t, tq, heads_per_step):
    n, c = q.shape
    dh = c // SB_HEADS
    nq = t // tq
    w = heads_per_step * dh
    kv_spec = pl.BlockSpec((1, t, w), lambda b, g, i: (b, 0, g))
    return pl.pallas_call(
        functools.partial(_sb_prompt_kernel, tq=tq, dh=dh, scale=dh ** -0.5),
        out_shape=jax.ShapeDtypeStruct((n, c), BF16),
        grid=(bsz, SB_HEADS // heads_per_step, nq),
        in_specs=[
            pl.BlockSpec((tq, w), lambda b, g, i: (b * nq + i, g)),
            kv_spec, kv_spec,
            pl.BlockSpec((tq, tq), lambda b, g, i: (0, 0)),
        ],
        out_specs=pl.BlockSpec((tq, w), lambda b, g, i: (b * nq + i, g)),
        compiler_params=_params("parallel", "parallel", "arbitrary"),
        name="sb_attention_prompt",
    )(q, kb.reshape(bsz, t, c), vb.reshape(bsz, t, c), tri)


def _sb_sample_kernel(q_ref, kn_ref, vn_ref, kp_ref, vp_ref, trin_ref, trip_ref, o_ref, *, tk, dh, scale):
    tq = q_ref.shape[0]
    cols = [slice(a * dh, (a + 1) * dh) for a in range(SB_HEADS)]
    qs = [q_ref[:, c] for c in cols]
    tri_new = trin_ref[...]
    tri_past = trip_ref[...]
    state = tuple((jnp.zeros((tq, 1), F32), jnp.zeros((tq, dh), F32)) for _ in range(SB_HEADS))
    state = _sb_blocks(qs, [kn_ref[:, c] for c in cols], [vn_ref[:, c] for c in cols], tri_new,
                       state, _causal_mask(tq), scale)
    n_past = kp_ref.shape[1] // (tk * SB_HEADS)
    for i in range(n_past):
        s0 = (n_past - 1 - i) * tk * SB_HEADS
        rows = [pl.ds(s0 + a, tk, stride=SB_HEADS) for a in range(SB_HEADS)]
        state = _sb_blocks(qs, [kp_ref[0, r, :].astype(BF16) for r in rows],
                           [vp_ref[0, r, :].astype(BF16) for r in rows], tri_past, state, None, scale)
    for a in range(SB_HEADS):
        o_ref[:, cols[a]] = state[a][1].astype(o_ref.dtype)


def _sb_attention_sample(q, kb_new, vb_new, k_past, v_past, tri_new, tri_past, *, bsz, t, tk):
    n, c = q.shape
    dh = c // SB_HEADS
    rows_past = k_past.shape[1]
    new_spec = pl.BlockSpec((t, c), lambda b: (b, 0))
    past_spec = pl.BlockSpec((1, rows_past, dh), lambda b: (b, 0, 0))
    return pl.pallas_call(
        functools.partial(_sb_sample_kernel, tk=tk, dh=dh, scale=dh ** -0.5),
        out_shape=jax.ShapeDtypeStruct((n, c), BF16),
        grid=(bsz,),
        in_specs=[
            new_spec, new_spec, new_spec, past_spec, past_spec,
            pl.BlockSpec((t, t), lambda b: (0, 0)),
            pl.BlockSpec((tk, tk), lambda b: (0, 0)),
        ],
        out_specs=new_spec,
        compiler_params=_params("parallel"),
        name="sb_attention_sample",
    )(q, kb_new, vb_new, k_past, v_past, tri_new, tri_past)


def _strict_lower_ones(n):
    idx = jnp.arange(n)
    return (idx[:, None] > idx[None, :]).astype(BF16)


def _out_proj_kernel(x_ref, yc_ref, ys_ref, wc_ref, ws_ref, g_ref, b_ref, o_ref, *, alpha):
    for rows in _row_parts(x_ref.shape[0]):
        mix = _dot(yc_ref[rows, :], wc_ref[...]) + _dot(ys_ref[rows, :], ws_ref[...])
        o_ref[rows, :] = _layer_norm(alpha * x_ref[rows, :] + mix, g_ref[...], b_ref[...])


def _out_proj_post_norm(x, y_conv, y_sb, w_out, g, b, *, alpha, tm):
    n, d = x.shape
    c = y_conv.shape[1]
    vec = pl.BlockSpec((1, d), lambda i: (0, 0))
    return pl.pallas_call(
        functools.partial(_out_proj_kernel, alpha=alpha),
        out_shape=jax.ShapeDtypeStruct((n, d), F32),
        grid=(n // tm,),
        in_specs=[
            pl.BlockSpec((tm, d), lambda i: (i, 0)),
            pl.BlockSpec((tm, c), lambda i: (i, 0)),
            pl.BlockSpec((tm, c), lambda i: (i, 0)),
            pl.BlockSpec((c, d), lambda i: (0, 0)),
            pl.BlockSpec((c, d), lambda i: (1, 0)),
            vec, vec,
        ],
        out_specs=pl.BlockSpec((tm, d), lambda i: (i, 0)),
        compiler_params=_params("parallel"),
        name="out_proj_post_norm",
    )(x, y_conv, y_sb, w_out, w_out, g, b)


def _mem_kv_kernel(m_ref, wk_ref, wv_ref, k_ref, v_ref):
    mb = m_ref[...].astype(BF16)
    k_ref[...] = _dot(mb, wk_ref[...])
    v_ref[...] = _dot(mb, wv_ref[...])


def _memory_kv(mem, wk, wv, *, tm):
    n, d = mem.shape
    c = wk.shape[1]
    w_spec = pl.BlockSpec((d, c), lambda i: (0, 0))
    o_spec = pl.BlockSpec((tm, c), lambda i: (i, 0))
    return pl.pallas_call(
        _mem_kv_kernel,
        out_shape=(jax.ShapeDtypeStruct((n, c), F32), jax.ShapeDtypeStruct((n, c), F32)),
        grid=(n // tm,),
        in_specs=[pl.BlockSpec((tm, d), lambda i: (i, 0)), w_spec, w_spec],
        out_specs=(o_spec, o_spec),
        compiler_params=_params("parallel"),
        name="memory_kv",
    )(mem, wk, wv)


def _xattn_kernel(x_ref, mk_ref, mv_ref, wq_ref, wo_ref, g_ref, b_ref, o_ref, *, alpha, scale):
    mk = mk_ref[0].astype(BF16)
    mv = mv_ref[0].astype(BF16)
    dh = mk.shape[1] // XATTN_HEADS
    for rows in _row_parts(x_ref.shape[0]):
        x = x_ref[rows, :]
        q = _dot(x.astype(BF16), wq_ref[...])
        heads = []
        for h in range(XATTN_HEADS):
            sl = slice(h * dh, (h + 1) * dh)
            s = _dot_nt(q[:, sl].astype(BF16), mk[:, sl]) * scale
            e = jnp.exp(s - jnp.max(s, axis=-1, keepdims=True))
            p = e / jnp.sum(e, axis=-1, keepdims=True)
            heads.append(_dot(p.astype(BF16), mv[:, sl]).astype(BF16))
        o = jnp.concatenate(heads, axis=1)
        y = alpha * x + _dot(o, wo_ref[...])
        o_ref[rows, :] = _layer_norm(y, g_ref[...], b_ref[...])


def _xattn_post_norm(x, mem_k, mem_v, wq, wo, g, b, *, alpha, bsz, t, tm):
    n, d = x.shape
    m, c = mem_k.shape[1], mem_k.shape[2]
    nt = t // tm
    vec = pl.BlockSpec((1, d), lambda bi, i: (0, 0))
    mem_spec = pl.BlockSpec((1, m, c), lambda bi, i: (bi, 0, 0))
    return pl.pallas_call(
        functools.partial(_xattn_kernel, alpha=alpha, scale=(c // XATTN_HEADS) ** -0.5),
        out_shape=jax.ShapeDtypeStruct((n, d), F32),
        grid=(bsz, nt),
        in_specs=[
            pl.BlockSpec((tm, d), lambda bi, i: (bi * nt + i, 0)),
            mem_spec, mem_spec,
            pl.BlockSpec((d, c), lambda bi, i: (0, 0)),
            pl.BlockSpec((c, d), lambda bi, i: (0, 0)),
            vec, vec,
        ],
        out_specs=pl.BlockSpec((tm, d), lambda bi, i: (bi * nt + i, 0)),
        compiler_params=_params("parallel", "parallel"),
        name="xattn_post_norm",
    )(x, mem_k, mem_v, wq, wo, g, b)


def _encoder_layer(x3, conv_prev, sb_past, mem_k, mem_v, w, *, alpha, tm, tt, tq):
    bsz, t, d = x3.shape
    x = x3.reshape(bsz * t, d)
    x = _ffn_post_norm(x, w["f1g"], w["f1u"], w["f1d"], w["ln1g"], w["ln1b"], alpha=alpha, tm=tm, tf=512)
    h, q, k, v, kb, vb = _in_proj(x, w["w_in"], tm=min(tm, IN_PROJ_ROWS))
    c = h.shape[1]
    y_conv, conv_new = _conv_module(h.reshape(bsz, t, c), conv_prev, w["conv_w"], w["conv_b"],
                                    w["cln_g"], w["cln_b"], tt=tt)
    y_conv = y_conv.reshape(bsz * t, c)
    if sb_past is None:
        y_sb = _sb_attention_prompt(q, kb, vb, _strict_lower_ones(tq), bsz=bsz, t=t, tq=tq, heads_per_step=4)
    else:
        k_past, v_past = sb_past
        y_sb = _sb_attention_sample(q, kb, vb, k_past, v_past, _strict_lower_ones(t), _strict_lower_ones(tq),
                                    bsz=bsz, t=t, tk=tq)
    x = _out_proj_post_norm(x, y_conv, y_sb, w["w_out"], w["ln2g"], w["ln2b"], alpha=alpha, tm=tm)
    x = _xattn_post_norm(x, mem_k, mem_v, w["xq"], w["xo"], w["ln3g"], w["ln3b"],
                         alpha=alpha, bsz=bsz, t=t, tm=min(tm, t))
    x = _ffn_post_norm(x, w["f2g"], w["f2u"], w["f2d"], w["ln4g"], w["ln4b"], alpha=alpha, tm=tm, tf=512)
    dh = c // SB_HEADS
    return (x.reshape(bsz, t, d), conv_new, k.reshape(bsz, t, SB_HEADS, dh), v.reshape(bsz, t, SB_HEADS, dh))


def kernel(x_prompt, x_sample, mem_prompt, cache_conv, cache_sb_k, cache_sb_v, cache_mem_k, cache_mem_v,
           ffn1_w_gate, ffn1_w_up, ffn1_w_down, ln1_g, ln1_b, w_in, conv_w, conv_b, conv_ln_g, conv_ln_b,
           w_out, ln2_g, ln2_b, xattn_wq, xattn_wk, xattn_wv, xattn_wo, ln3_g, ln3_b,
           ffn2_w_gate, ffn2_w_up, ffn2_w_down, ln4_g, ln4_b):
    depth = ffn1_w_gate.shape[0]
    alpha = (2.0 * depth) ** 0.25
    bp, tp, d = x_prompt.shape
    bs, ts, _ = x_sample.shape
    conv_dim = conv_w.shape[2]
    xattn_dim = xattn_wq.shape[2]

    xp, xs = x_prompt, x_sample
    conv_p, k_p, v_p, mk_p, mv_p, conv_s, k_s, v_s = [], [], [], [], [], [], [], []
    for l in range(depth):
        row = lambda a: a[l][None, :]
        w = dict(
            f1g=ffn1_w_gate[l].astype(BF16), f1u=ffn1_w_up[l].astype(BF16), f1d=ffn1_w_down[l].astype(BF16),
            ln1g=row(ln1_g), ln1b=row(ln1_b), w_in=w_in[l].astype(BF16),
            conv_w=conv_w[l], conv_b=row(conv_b), cln_g=row(conv_ln_g), cln_b=row(conv_ln_b),
            w_out=w_out[l].astype(BF16), ln2g=row(ln2_g), ln2b=row(ln2_b),
            xq=xattn_wq[l].astype(BF16), xo=xattn_wo[l].astype(BF16), ln3g=row(ln3_g), ln3b=row(ln3_b),
            f2g=ffn2_w_gate[l].astype(BF16), f2u=ffn2_w_up[l].astype(BF16), f2d=ffn2_w_down[l].astype(BF16),
            ln4g=row(ln4_g), ln4b=row(ln4_b),
        )
        m = mem_prompt.shape[1]
        mk, mv = _memory_kv(mem_prompt.reshape(bp * m, d), xattn_wk[l].astype(BF16), xattn_wv[l].astype(BF16),
                            tm=512)
        mk = mk.reshape(bp, m, xattn_dim)
        mv = mv.reshape(bp, m, xattn_dim)
        xp, cp, kp, vp = _encoder_layer(xp, jnp.zeros((bp, CONV_STATE, conv_dim), F32), None, mk, mv, w,
                                        alpha=alpha, tm=512, tt=256, tq=256)
        conv_p.append(cp); k_p.append(kp); v_p.append(vp)
        mk_p.append(mk.reshape(bp, m, XATTN_HEADS, xattn_dim // XATTN_HEADS))
        mv_p.append(mv.reshape(bp, m, XATTN_HEADS, xattn_dim // XATTN_HEADS))

        p_len = cache_sb_k.shape[2]
        dh = cache_sb_k.shape[4]
        sb_past = (cache_sb_k[l].reshape(bs, p_len * SB_HEADS, dh), cache_sb_v[l].reshape(bs, p_len * SB_HEADS, dh))
        xs, cs, ks, vs = _encoder_layer(xs, cache_conv[l], sb_past,
                                        cache_mem_k[l].reshape(bs, m, xattn_dim),
                                        cache_mem_v[l].reshape(bs, m, xattn_dim), w,
                                        alpha=alpha, tm=bs * ts, tt=ts, tq=256)
        conv_s.append(cs); k_s.append(ks); v_s.append(vs)
    stack = lambda parts: parts[0][None] if len(parts) == 1 else jnp.stack(parts)
    return (xp, xs, stack(conv_p), stack(k_p), stack(v_p), stack(mk_p), stack(mv_p),
            stack(conv_s), stack(k_s), stack(v_s))
```

```python
import functools

import jax
import jax.numpy as jnp
from jax import lax
from jax.experimental import pallas as pl
from jax.experimental.pallas import tpu as pltpu

F32 = jnp.float32
BF16 = jnp.bfloat16

LN_EPS = 1e-5
CONV_WIDTH = 31
CONV_STATE = CONV_WIDTH - 1
SB_HEADS = 8
XATTN_HEADS = 4
CONV_CTX_ROWS = 32
CONV_ROW_CHUNK = 32
CONV_SUBLANES = 8
MXU_ROWS = 256
CAST_SLAB_ROWS = 16
IN_PROJ_ROWS = 256
SB_CARRY_CUTOFF = 110.0
VMEM_LIMIT_BYTES = 56 * 1024 * 1024


def _layer_norm(y, g, b):
    mu = jnp.mean(y, axis=-1, keepdims=True)
    d = y - mu
    var = jnp.mean(d * d, axis=-1, keepdims=True)
    return d * lax.rsqrt(var + LN_EPS) * g + b


def _dot(a, b):
    return jnp.dot(a, b, preferred_element_type=F32)


def _dot_nt(a, b):
    return lax.dot_general(a, b, (((1,), (1,)), ((), ())), preferred_element_type=F32)


def _row_parts(tm):
    parts = 2 if tm % (2 * MXU_ROWS) == 0 else 1
    return [slice(r * tm // parts, (r + 1) * tm // parts) for r in range(parts)]


def _params(*sem, flags=None):
    return pltpu.CompilerParams(dimension_semantics=sem, vmem_limit_bytes=VMEM_LIMIT_BYTES, flags=flags)


def _ffn_kernel(x_ref, wg_ref, wu_ref, wd_ref, g_ref, b_ref, *rest, alpha, nf, n_cast):
    cast_in, o_ref, cast_out, acc_ref = rest[:n_cast], rest[n_cast], rest[n_cast + 1:-1], rest[-1]
    f = pl.program_id(1)

    def step(first, last):
        for src, dst in zip(cast_in, cast_out):
            dst[...] = src[...].astype(BF16)
        tm = x_ref.shape[0]
        parts = 2 if last else 1
        for r in range(parts):
            rows = slice(r * tm // parts, (r + 1) * tm // parts)
            x = x_ref[rows, :]
            xb = x.astype(BF16)
            gate = _dot(xb, wg_ref[...])
            up = _dot(xb, wu_ref[...])
            h = (gate * jax.nn.sigmoid(gate) * up).astype(BF16)
            total = _dot(h, wd_ref[...])
            if not first:
                total = acc_ref[rows, :] + total
            if last:
                o_ref[rows, :] = _layer_norm(alpha * x + 0.5 * total, g_ref[...], b_ref[...])
            else:
                acc_ref[rows, :] = total

    if nf == 1:
        step(True, True)
    else:
        pl.when(f == 0)(lambda: step(True, False))
        pl.when(jnp.logical_and(f > 0, f < nf - 1))(lambda: step(False, False))
        pl.when(f == nf - 1)(lambda: step(False, True))


def _cast_slab_spec(rows, cols, n_tiles, nf):
    per_tile = rows // CAST_SLAB_ROWS // n_tiles
    assert per_tile * CAST_SLAB_ROWS * n_tiles == rows and 1 <= per_tile <= nf, (rows, n_tiles, nf)
    return pl.BlockSpec((CAST_SLAB_ROWS, cols), lambda i, f: (i * per_tile + jnp.minimum(f, per_tile - 1), 0))


def _ffn_post_norm(x, wg, wu, wd, g, b, *, alpha, tm, tf, cast=()):
    n, d = x.shape
    dff = wg.shape[1]
    nf = dff // tf
    cast_specs = [_cast_slab_spec(w.shape[0], w.shape[1], n // tm, nf) for w in cast]
    out = pl.pallas_call(
        functools.partial(_ffn_kernel, alpha=alpha, nf=nf, n_cast=len(cast)),
        out_shape=(jax.ShapeDtypeStruct((n, d), F32), *[jax.ShapeDtypeStruct(w.shape, BF16) for w in cast]),
        grid=(n // tm, nf),
        in_specs=[
            pl.BlockSpec((tm, d), lambda i, f: (i, 0)),
            pl.BlockSpec((d, tf), lambda i, f: (0, f)),
            pl.BlockSpec((d, tf), lambda i, f: (0, f)),
            pl.BlockSpec((tf, d), lambda i, f: (f, 0)),
            pl.BlockSpec((1, d), lambda i, f: (0, 0)),
            pl.BlockSpec((1, d), lambda i, f: (0, 0)),
            *cast_specs,
        ],
        out_specs=(pl.BlockSpec((tm, d), lambda i, f: (i, 0)), *cast_specs),
        scratch_shapes=[pltpu.VMEM((tm, d), F32)],
        compiler_params=_params("arbitrary", "arbitrary"),
        name="ffn_post_norm",
    )(x, wg, wu, wd, g, b, *cast)
    return out[0], out[1:]


def _store_heads_interleaved(dst_ref, r):
    n, c = r.shape
    dh = c // SB_HEADS
    for h in range(SB_HEADS):
        dst_ref[pl.ds(h, n, stride=SB_HEADS), :] = r[:, h * dh:(h + 1) * dh]


def _in_proj_conv_kernel(x_ref, w_ref, prev_ref, cw_ref, cb_ref, g_ref, b_ref,
                         y_ref, st_ref, q_ref, k_ref, v_ref, kb_ref, vb_ref,
                         tail_ref, hp_ref, sh_ref, *, seg, steps_per_seq):
    ctx, sub, rc = CONV_CTX_ROWS, CONV_SUBLANES, CONV_ROW_CHUNK
    pad = ctx - CONV_STATE
    tm = x_ref.shape[0]
    c = y_ref.shape[1]
    nseg = tm // seg
    stride = ctx + seg

    def load_cached_context():
        for s in range(nseg):
            tail_ref[s * ctx:s * ctx + pad, :] = jnp.zeros((pad, c), F32)
            tail_ref[s * ctx + pad:(s + 1) * ctx, :] = prev_ref[s]

    if steps_per_seq == 1:
        load_cached_context()
    else:
        pl.when(pl.program_id(0) % steps_per_seq == 0)(load_cached_context)

    xb = x_ref[...].astype(BF16)
    group = lambda j: _dot(xb, w_ref[:, j * c:(j + 1) * c])
    h = group(0) * jax.nn.sigmoid(group(1))
    for s in range(nseg):
        hp_ref[s * stride:s * stride + ctx, :] = tail_ref[s * ctx:(s + 1) * ctx, :]
        hp_ref[s * stride + ctx:(s + 1) * stride, :] = h[s * seg:(s + 1) * seg, :]

    dh = k_ref.shape[1]

    def qkv_tile(j, l0):
        lanes = slice(l0, l0 + MXU_ROWS)
        r = _dot(xb, w_ref[:, j * c + l0:j * c + l0 + MXU_ROWS])
        if j == 2:
            q_ref[:, lanes] = r.astype(BF16)
            return
        state_ref, bf_ref = (k_ref, kb_ref) if j == 3 else (v_ref, vb_ref)
        for i in range(MXU_ROWS // dh):
            state_ref[pl.ds(l0 // dh + i, tm, stride=SB_HEADS), :] = r[:, i * dh:(i + 1) * dh]
        bf_ref[:, lanes] = r.astype(BF16)

    qkv_tiles = [(j, l0) for j in (2, 3, 4) for l0 in range(0, c, MXU_ROWS)]

    span = nseg * stride - sub
    for r in range(1, sub):
        sh_ref[r - 1, 0:span, :] = hp_ref[r:r + span, :]

    for tile in qkv_tiles:
        qkv_tile(*tile)

    groups = rc // sub
    for s in range(nseg):
        for ci in range(seg // rc):
            r0 = s * stride + ci * rc
            acc = [jnp.broadcast_to(cb_ref[...], (sub, c)) for _ in range(groups)]
            for j in range(CONV_WIDTH):
                a, r = divmod(pad + j, sub)
                wj = cw_ref[j]
                for gi in range(groups):
                    lo = r0 + (a + gi) * sub
                    rows = hp_ref[lo:lo + sub, :] if r == 0 else sh_ref[r - 1, lo:lo + sub, :]
                    acc[gi] = acc[gi] + wj * rows
            y = _layer_norm(jnp.concatenate(acc, axis=0), g_ref[...], b_ref[...])
            y_ref[s * seg + ci * rc:s * seg + (ci + 1) * rc, :] = (y * jax.nn.sigmoid(y)).astype(y_ref.dtype)

    for s in range(nseg):
        last = hp_ref[s * stride + seg:(s + 1) * stride, :]
        tail_ref[s * ctx:(s + 1) * ctx, :] = last
        st_ref[s] = last[pad:, :]


def _in_proj_conv(x, w_in, prev, conv_w, conv_b, ln_g, ln_b, *, tm, seg, steps_per_seq):
    n, d = x.shape
    cols = w_in.shape[1]
    c = cols // 5
    dh = c // SB_HEADS
    nseg = tm // seg
    nreq = prev.shape[0]
    hp_rows = nseg * (CONV_CTX_ROWS + seg)
    out_block = pl.BlockSpec((tm, c), lambda i: (i, 0))
    state_block = pl.BlockSpec((tm * SB_HEADS, dh), lambda i: (i, 0))
    conv_state_block = pl.BlockSpec((nseg, CONV_STATE, c), lambda i: (i // steps_per_seq, 0, 0))
    vec = pl.BlockSpec((1, c), lambda i: (0, 0))
    return pl.pallas_call(
        functools.partial(_in_proj_conv_kernel, seg=seg, steps_per_seq=steps_per_seq),
        out_shape=(
            jax.ShapeDtypeStruct((n, c), BF16),
            jax.ShapeDtypeStruct((nreq, CONV_STATE, c), F32),
            jax.ShapeDtypeStruct((n, c), BF16),
            jax.ShapeDtypeStruct((n * SB_HEADS, dh), F32),
            jax.ShapeDtypeStruct((n * SB_HEADS, dh), F32),
            jax.ShapeDtypeStruct((n, c), BF16),
            jax.ShapeDtypeStruct((n, c), BF16),
        ),
        grid=(n // tm,),
        in_specs=[
            pl.BlockSpec((tm, d), lambda i: (i, 0)),
            pl.BlockSpec((d, cols), lambda i: (0, 0), pipeline_mode=pl.Buffered(1)),
            conv_state_block,
            pl.BlockSpec((CONV_WIDTH, CONV_SUBLANES, c), lambda i: (0, 0, 0)),
            vec, vec, vec,
        ],
        out_specs=(out_block, conv_state_block, out_block, state_block, state_block, out_block, out_block),
        scratch_shapes=[pltpu.VMEM((nseg * CONV_CTX_ROWS, c), F32),
                        pltpu.VMEM((hp_rows, c), F32),
                        pltpu.VMEM((CONV_SUBLANES - 1, hp_rows - CONV_SUBLANES, c), F32)],
        compiler_params=_params("arbitrary", ),
        name="in_proj_conv",
    )(x, w_in, prev, jnp.broadcast_to(conv_w[:, None, :], (CONV_WIDTH, CONV_SUBLANES, c)), conv_b, ln_g, ln_b)


def _sb_blocks(qs, ks, vs, tri, state, mask, scale):
    n = len(qs)
    tq = qs[0].shape[0]
    zs = [_dot_nt(qs[a], ks[a]) * scale for a in range(n)]
    sps = []
    for z in zs:
        sp = jnp.maximum(z, 0.0) + jnp.log(1.0 + jnp.exp(-jnp.abs(z)))
        sps.append(sp if mask is None else jnp.where(mask, sp, 0.0))
    parts = []
    for sp in sps:
        hi = sp.astype(BF16)
        parts += [hi, (sp - hi.astype(F32)).astype(BF16)]
    sums = _dot(jnp.concatenate(parts, axis=0), tri)
    out = []
    for a in range(n):
        carry, acc = state[a]
        later = sums[2 * a * tq:(2 * a + 1) * tq] + sums[(2 * a + 1) * tq:(2 * a + 2) * tq]
        w = jnp.exp(zs[a] - sps[a] - later - carry)
        if mask is not None:
            w = jnp.where(mask, w, 0.0)
        out.append((carry + jnp.sum(sps[a], axis=1, keepdims=True), acc + _dot(w.astype(BF16), vs[a])))
    return tuple(out)


def _causal_mask(n):
    row = lax.broadcasted_iota(jnp.int32, (n, n), 0)
    col = lax.broadcasted_iota(jnp.int32, (n, n), 1)
    return col < row


def _sb_prompt_kernel(q_ref, k_ref, v_ref, tri_ref, o_ref, *, tq, dh, scale):
    qi = pl.program_id(2)
    heads = q_ref.shape[1] // dh
    tri = tri_ref[...]
    cols = [slice(a * dh, (a + 1) * dh) for a in range(heads)]
    qs = [q_ref[:, c] for c in cols]

    def visit(s0, state, mask):
        ks = [k_ref[0, pl.ds(s0, tq), c] for c in cols]
        vs = [v_ref[0, pl.ds(s0, tq), c] for c in cols]
        return _sb_blocks(qs, ks, vs, tri, state, mask, scale)

    state = tuple((jnp.zeros((tq, 1), F32), jnp.zeros((tq, dh), F32)) for _ in range(heads))
    state = visit(pl.multiple_of(qi * tq, tq), state, _causal_mask(tq))

    def live(st):
        lowest = functools.reduce(jnp.minimum, [jnp.min(c) for c, _ in st])
        return lowest < SB_CARRY_CUTOFF

    def cond(loop):
        i, more, _ = loop
        return jnp.logical_and(i < qi, more)

    def body(loop):
        i, _, st = loop
        st = visit(pl.multiple_of((qi - 1 - i) * tq, tq), st, None)
        return i + 1, live(st), st

    state = lax.while_loop(cond, body, (jnp.int32(0), live(state), state))[2]
    for a in range(heads):
        o_ref[:, cols[a]] = state[a][1].astype(o_ref.dtype)


def _sb_attention_prompt(q, kb, vb, tri, *, bsz, t, tq, heads_per_step):
    n, c = q.shape
    dh = c // SB_HEADS
    nq = t // tq
    w = heads_per_step * dh
    kv_spec = pl.BlockSpec((1, t, w), lambda b, g, i: (b, 0, g))
    return pl.pallas_call(
        functools.partial(_sb_prompt_kernel, tq=tq, dh=dh, scale=dh ** -0.5),
        out_shape=jax.ShapeDtypeStruct((n, c), BF16),
        grid=(bsz, SB_HEADS // heads_per_step, nq),
        in_specs=[
            pl.BlockSpec((tq, w), lambda b, g, i: (b * nq + i, g)),
            kv_spec, kv_spec,
            pl.BlockSpec((tq, tq), lambda b, g, i: (0, 0)),
        ],
        out_specs=pl.BlockSpec((tq, w), lambda b, g, i: (b * nq + i, g)),
        compiler_params=_params("parallel", "parallel", "arbitrary"),
        name="sb_attention_prompt",
    )(q, kb.reshape(bsz, t, c), vb.reshape(bsz, t, c), tri)


def _sb_sample_kernel(q_ref, kn_ref, vn_ref, kp_ref, vp_ref, trin_ref, trip_ref, o_ref, *, tk, dh, scale):
    tq = q_ref.shape[0]
    cols = [slice(a * dh, (a + 1) * dh) for a in range(SB_HEADS)]
    qs = [q_ref[:, c] for c in cols]
    tri_new = trin_ref[...]
    tri_past = trip_ref[...]
    state = tuple((jnp.zeros((tq, 1), F32), jnp.zeros((tq, dh), F32)) for _ in range(SB_HEADS))
    state = _sb_blocks(qs, [kn_ref[:, c] for c in cols], [vn_ref[:, c] for c in cols], tri_new,
                       state, _causal_mask(tq), scale)
    n_past = kp_ref.shape[1] // (tk * SB_HEADS)
    for i in range(n_past):
        s0 = (n_past - 1 - i) * tk * SB_HEADS
        rows = [pl.ds(s0 + a, tk, stride=SB_HEADS) for a in range(SB_HEADS)]
        state = _sb_blocks(qs, [kp_ref[0, r, :].astype(BF16) for r in rows],
                           [vp_ref[0, r, :].astype(BF16) for r in rows], tri_past, state, None, scale)
    for a in range(SB_HEADS):
        o_ref[:, cols[a]] = state[a][1].astype(o_ref.dtype)


def _sb_attention_sample(q, kb_new, vb_new, k_past, v_past, tri_new, tri_past, *, bsz, t, tk):
    n, c = q.shape
    dh = c // SB_HEADS
    rows_past = k_past.shape[1]
    new_spec = pl.BlockSpec((t, c), lambda b: (b, 0))
    past_spec = pl.BlockSpec((1, rows_past, dh), lambda b: (b, 0, 0))
    return pl.pallas_call(
        functools.partial(_sb_sample_kernel, tk=tk, dh=dh, scale=dh ** -0.5),
        out_shape=jax.ShapeDtypeStruct((n, c), BF16),
        grid=(bsz,),
        in_specs=[
            new_spec, new_spec, new_spec, past_spec, past_spec,
            pl.BlockSpec((t, t), lambda b: (0, 0)),
            pl.BlockSpec((tk, tk), lambda b: (0, 0)),
        ],
        out_specs=new_spec,
        compiler_params=_params("parallel"),
        name="sb_attention_sample",
    )(q, kb_new, vb_new, k_past, v_past, tri_new, tri_past)


def _strict_lower_ones(n):
    idx = jnp.arange(n)
    return (idx[:, None] > idx[None, :]).astype(BF16)


def _out_proj_kernel(x_ref, yc_ref, ys_ref, wc_ref, ws_ref, g_ref, b_ref, o_ref, *, alpha):
    for rows in _row_parts(x_ref.shape[0]):
        mix = _dot(yc_ref[rows, :], wc_ref[...]) + _dot(ys_ref[rows, :], ws_ref[...])
        o_ref[rows, :] = _layer_norm(alpha * x_ref[rows, :] + mix, g_ref[...], b_ref[...])


def _out_proj_post_norm(x, y_conv, y_sb, w_out, g, b, *, alpha, tm):
    n, d = x.shape
    c = y_conv.shape[1]
    vec = pl.BlockSpec((1, d), lambda i: (0, 0))
    return pl.pallas_call(
        functools.partial(_out_proj_kernel, alpha=alpha),
        out_shape=jax.ShapeDtypeStruct((n, d), F32),
        grid=(n // tm,),
        in_specs=[
            pl.BlockSpec((tm, d), lambda i: (i, 0)),
            pl.BlockSpec((tm, c), lambda i: (i, 0)),
            pl.BlockSpec((tm, c), lambda i: (i, 0)),
            pl.BlockSpec((c, d), lambda i: (0, 0)),
            pl.BlockSpec((c, d), lambda i: (1, 0)),
            vec, vec,
        ],
        out_specs=pl.BlockSpec((tm, d), lambda i: (i, 0)),
        compiler_params=_params("parallel"),
        name="out_proj_post_norm",
    )(x, y_conv, y_sb, w_out, w_out, g, b)


def _mem_kv_kernel(m_ref, wk_ref, wv_ref, k_ref, v_ref):
    mb = m_ref[...].astype(BF16)
    k_ref[...] = _dot(mb, wk_ref[...])
    v_ref[...] = _dot(mb, wv_ref[...])


def _memory_kv(mem, wk, wv, *, tm):
    n, d = mem.shape
    c = wk.shape[1]
    w_spec = pl.BlockSpec((d, c), lambda i: (0, 0))
    o_spec = pl.BlockSpec((tm, c), lambda i: (i, 0))
    return pl.pallas_call(
        _mem_kv_kernel,
        out_shape=(jax.ShapeDtypeStruct((n, c), F32), jax.ShapeDtypeStruct((n, c), F32)),
        grid=(n // tm,),
        in_specs=[pl.BlockSpec((tm, d), lambda i: (i, 0)), w_spec, w_spec],
        out_specs=(o_spec, o_spec),
        compiler_params=_params("parallel"),
        name="memory_kv",
    )(mem, wk, wv)


def _xattn_kernel(x_ref, mk_ref, mv_ref, wq_ref, wo_ref, g_ref, b_ref, o_ref, *, alpha, scale):
    mk = mk_ref[0].astype(BF16)
    mv = mv_ref[0].astype(BF16)
    dh = mk.shape[1] // XATTN_HEADS
    for rows in _row_parts(x_ref.shape[0]):
        x = x_ref[rows, :]
        q = _dot(x.astype(BF16), wq_ref[...])
        heads = []
        for h in range(XATTN_HEADS):
            sl = slice(h * dh, (h + 1) * dh)
            s = _dot_nt(q[:, sl].astype(BF16), mk[:, sl]) * scale
            e = jnp.exp(s - jnp.max(s, axis=-1, keepdims=True))
            p = e / jnp.sum(e, axis=-1, keepdims=True)
            heads.append(_dot(p.astype(BF16), mv[:, sl]).astype(BF16))
        o = jnp.concatenate(heads, axis=1)
        y = alpha * x + _dot(o, wo_ref[...])
        o_ref[rows, :] = _layer_norm(y, g_ref[...], b_ref[...])


def _xattn_post_norm(x, mem_k, mem_v, wq, wo, g, b, *, alpha, bsz, t, tm):
    n, d = x.shape
    m, c = mem_k.shape[1], mem_k.shape[2]
    nt = t // tm
    vec = pl.BlockSpec((1, d), lambda bi, i: (0, 0))
    mem_spec = pl.BlockSpec((1, m, c), lambda bi, i: (bi, 0, 0))
    return pl.pallas_call(
        functools.partial(_xattn_kernel, alpha=alpha, scale=(c // XATTN_HEADS) ** -0.5),
        out_shape=jax.ShapeDtypeStruct((n, d), F32),
        grid=(bsz, nt),
        in_specs=[
            pl.BlockSpec((tm, d), lambda bi, i: (bi * nt + i, 0)),
            mem_spec, mem_spec,
            pl.BlockSpec((d, c), lambda bi, i: (0, 0)),
            pl.BlockSpec((c, d), lambda bi, i: (0, 0)),
            vec, vec,
        ],
        out_specs=pl.BlockSpec((tm, d), lambda bi, i: (bi * nt + i, 0)),
        compiler_params=_params("parallel", "parallel"),
        name="xattn_post_norm",
    )(x, mem_k, mem_v, wq, wo, g, b)


LATER_MATRICES = ("w_in", "w_out", "xq", "xo", "f2g", "f2u", "f2d")


def _encoder_layer(x3, conv_prev, sb_past, mem_k, mem_v, w, *, alpha, tm, tq):
    bsz, t, d = x3.shape
    x = x3.reshape(bsz * t, d)
    pending = [name for name in LATER_MATRICES if w[name].dtype != BF16]
    x, cast = _ffn_post_norm(x, w["f1g"], w["f1u"], w["f1d"], w["ln1g"], w["ln1b"], alpha=alpha, tm=tm, tf=512,
                             cast=tuple(w[name] for name in pending))
    w = {**w, **dict(zip(pending, cast))}
    seg = min(t, IN_PROJ_ROWS)
    y_conv, conv_new, q, k, v, kb, vb = _in_proj_conv(
        x, w["w_in"], conv_prev, w["conv_w"], w["conv_b"], w["cln_g"], w["cln_b"],
        tm=IN_PROJ_ROWS, seg=seg, steps_per_seq=t // seg)
    c = y_conv.shape[1]
    if sb_past is None:
        y_sb = _sb_attention_prompt(q, kb, vb, _strict_lower_ones(tq), bsz=bsz, t=t, tq=tq, heads_per_step=8)
    else:
        k_past, v_past = sb_past
        y_sb = _sb_attention_sample(q, kb, vb, k_past, v_past, _strict_lower_ones(t), _strict_lower_ones(tq),
                                    bsz=bsz, t=t, tk=tq)
    x = _out_proj_post_norm(x, y_conv, y_sb, w["w_out"], w["ln2g"], w["ln2b"], alpha=alpha, tm=tm)
    x = _xattn_post_norm(x, mem_k, mem_v, w["xq"], w["xo"], w["ln3g"], w["ln3b"],
                         alpha=alpha, bsz=bsz, t=t, tm=min(tm, t))
    x, _ = _ffn_post_norm(x, w["f2g"], w["f2u"], w["f2d"], w["ln4g"], w["ln4b"], alpha=alpha, tm=tm, tf=512)
    dh = c // SB_HEADS
    return (x.reshape(bsz, t, d), conv_new, k.reshape(bsz, t, SB_HEADS, dh), v.reshape(bsz, t, SB_HEADS, dh)), w


def kernel(x_prompt, x_sample, mem_prompt, cache_conv, cache_sb_k, cache_sb_v, cache_mem_k, cache_mem_v,
           ffn1_w_gate, ffn1_w_up, ffn1_w_down, ln1_g, ln1_b, w_in, conv_w, conv_b, conv_ln_g, conv_ln_b,
           w_out, ln2_g, ln2_b, xattn_wq, xattn_wk, xattn_wv, xattn_wo, ln3_g, ln3_b,
           ffn2_w_gate, ffn2_w_up, ffn2_w_down, ln4_g, ln4_b):
    depth = ffn1_w_gate.shape[0]
    alpha = (2.0 * depth) ** 0.25
    bp, tp, d = x_prompt.shape
    bs, ts, _ = x_sample.shape
    conv_dim = conv_w.shape[2]
    xattn_dim = xattn_wq.shape[2]

    xp, xs = x_prompt, x_sample
    conv_p, k_p, v_p, mk_p, mv_p, conv_s, k_s, v_s = [], [], [], [], [], [], [], []
    for l in range(depth):
        row = lambda a: a[l][None, :]
        w = dict(
            f1g=ffn1_w_gate[l].astype(BF16), f1u=ffn1_w_up[l].astype(BF16), f1d=ffn1_w_down[l].astype(BF16),
            ln1g=row(ln1_g), ln1b=row(ln1_b), w_in=w_in[l],
            conv_w=conv_w[l], conv_b=row(conv_b), cln_g=row(conv_ln_g), cln_b=row(conv_ln_b),
            w_out=w_out[l], ln2g=row(ln2_g), ln2b=row(ln2_b),
            xq=xattn_wq[l], xo=xattn_wo[l], ln3g=row(ln3_g), ln3b=row(ln3_b),
            f2g=ffn2_w_gate[l], f2u=ffn2_w_up[l], f2d=ffn2_w_down[l],
            ln4g=row(ln4_g), ln4b=row(ln4_b),
        )
        m = mem_prompt.shape[1]
        mk, mv = _memory_kv(mem_prompt.reshape(bp * m, d), xattn_wk[l].astype(BF16), xattn_wv[l].astype(BF16),
                            tm=512)
        mk = mk.reshape(bp, m, xattn_dim)
        mv = mv.reshape(bp, m, xattn_dim)
        (xp, cp, kp, vp), w = _encoder_layer(xp, jnp.zeros((bp, CONV_STATE, conv_dim), F32), None, mk, mv, w,
                                             alpha=alpha, tm=512, tq=256)
        conv_p.append(cp); k_p.append(kp); v_p.append(vp)
        mk_p.append(mk.reshape(bp, m, XATTN_HEADS, xattn_dim // XATTN_HEADS))
        mv_p.append(mv.reshape(bp, m, XATTN_HEADS, xattn_dim // XATTN_HEADS))

        p_len = cache_sb_k.shape[2]
        dh = cache_sb_k.shape[4]
        sb_past = (cache_sb_k[l].reshape(bs, p_len * SB_HEADS, dh), cache_sb_v[l].reshape(bs, p_len * SB_HEADS, dh))
        (xs, cs, ks, vs), _ = _encoder_layer(xs, cache_conv[l], sb_past,
                                             cache_mem_k[l].reshape(bs, m, xattn_dim),
                                             cache_mem_v[l].reshape(bs, m, xattn_dim), w,
                                             alpha=alpha, tm=bs * ts, tq=256)
        conv_s.append(cs); k_s.append(ks); v_s.append(vs)
    stack = lambda parts: parts[0][None] if len(parts) == 1 else jnp.stack(parts)
    return (xp, xs, stack(conv_p), stack(k_p), stack(v_p), stack(mk_p), stack(mv_p),
            stack(conv_s), stack(k_s), stack(v_s))
```

```python
import functools

import jax
import jax.numpy as jnp
from jax import lax
from jax.experimental import pallas as pl
from jax.experimental.pallas import tpu as pltpu

F32 = jnp.float32
BF16 = jnp.bfloat16

LN_EPS = 1e-5
CONV_WIDTH = 31
CONV_STATE = CONV_WIDTH - 1
SB_HEADS = 8
XATTN_HEADS = 4
CONV_CTX_ROWS = 32
CONV_ROW_CHUNK = 32
CONV_SUBLANES = 8
MXU_ROWS = 256
CAST_SLAB_ROWS = 16
IN_PROJ_ROWS = 256
SB_CARRY_CUTOFF = 110.0
VMEM_LIMIT_BYTES = 56 * 1024 * 1024
FFN_VMEM_LIMIT_BYTES = 60 * 1024 * 1024
FFN_ROWS = 1024


def _layer_norm(y, g, b):
    mu = jnp.mean(y, axis=-1, keepdims=True)
    d = y - mu
    var = jnp.mean(d * d, axis=-1, keepdims=True)
    return d * lax.rsqrt(var + LN_EPS) * g + b


def _dot(a, b):
    return jnp.dot(a, b, preferred_element_type=F32)


def _dot_nt(a, b):
    return lax.dot_general(a, b, (((1,), (1,)), ((), ())), preferred_element_type=F32)


def _row_parts(tm):
    parts = 2 if tm % (2 * MXU_ROWS) == 0 else 1
    return [slice(r * tm // parts, (r + 1) * tm // parts) for r in range(parts)]


def _params(*sem, flags=None):
    return pltpu.CompilerParams(dimension_semantics=sem, vmem_limit_bytes=VMEM_LIMIT_BYTES, flags=flags)


def _ffn_kernel(x_ref, wg_ref, wu_ref, wd_ref, g_ref, b_ref, *rest, alpha, nf, n_cast):
    cast_in, o_ref, cast_out, acc_ref = rest[:n_cast], rest[n_cast], rest[n_cast + 1:-1], rest[-1]
    f = pl.program_id(1)

    def step(first, last):
        for src, dst in zip(cast_in, cast_out):
            dst[...] = src[...].astype(BF16)
        tm = x_ref.shape[0]
        parts = max(1, tm // MXU_ROWS) if last else 1
        for r in range(parts):
            rows = slice(r * tm // parts, (r + 1) * tm // parts)
            x = x_ref[rows, :]
            xb = x.astype(BF16)
            gate = _dot(xb, wg_ref[...])
            up = _dot(xb, wu_ref[...])
            h = (gate * jax.nn.sigmoid(gate) * up).astype(BF16)
            total = _dot(h, wd_ref[...])
            if not first:
                total = acc_ref[rows, :] + total
            if last:
                o_ref[rows, :] = _layer_norm(alpha * x + 0.5 * total, g_ref[...], b_ref[...])
            else:
                acc_ref[rows, :] = total

    if nf == 1:
        step(True, True)
    else:
        pl.when(f == 0)(lambda: step(True, False))
        pl.when(jnp.logical_and(f > 0, f < nf - 1))(lambda: step(False, False))
        pl.when(f == nf - 1)(lambda: step(False, True))


def _cast_slab_spec(rows, cols, n_tiles, nf):
    slab = CAST_SLAB_ROWS
    while rows // slab // n_tiles > nf:
        slab += CAST_SLAB_ROWS
    per_tile = rows // slab // n_tiles
    assert per_tile * slab * n_tiles == rows and per_tile >= 1, (rows, n_tiles, nf)
    return pl.BlockSpec((slab, cols), lambda i, f: (i * per_tile + jnp.minimum(f, per_tile - 1), 0))


def _ffn_post_norm(x, wg, wu, wd, g, b, *, alpha, tm, tf, cast=()):
    n, d = x.shape
    dff = wg.shape[1]
    nf = dff // tf
    cast_specs = [_cast_slab_spec(w.shape[0], w.shape[1], n // tm, nf) for w in cast]
    out = pl.pallas_call(
        functools.partial(_ffn_kernel, alpha=alpha, nf=nf, n_cast=len(cast)),
        out_shape=(jax.ShapeDtypeStruct((n, d), F32), *[jax.ShapeDtypeStruct(w.shape, BF16) for w in cast]),
        grid=(n // tm, nf),
        in_specs=[
            pl.BlockSpec((tm, d), lambda i, f: (i, 0)),
            pl.BlockSpec((d, tf), lambda i, f: (0, f)),
            pl.BlockSpec((d, tf), lambda i, f: (0, f)),
            pl.BlockSpec((tf, d), lambda i, f: (f, 0)),
            pl.BlockSpec((1, d), lambda i, f: (0, 0)),
            pl.BlockSpec((1, d), lambda i, f: (0, 0)),
            *cast_specs,
        ],
        out_specs=(pl.BlockSpec((tm, d), lambda i, f: (i, 0), pipeline_mode=pl.Buffered(1)), *cast_specs),
        scratch_shapes=[pltpu.VMEM((tm, d), F32)],
        compiler_params=pltpu.CompilerParams(dimension_semantics=("arbitrary", "arbitrary"),
                                             vmem_limit_bytes=FFN_VMEM_LIMIT_BYTES),
        name="ffn_post_norm",
    )(x, wg, wu, wd, g, b, *cast)
    return out[0], out[1:]


def _store_heads_interleaved(dst_ref, r):
    n, c = r.shape
    dh = c // SB_HEADS
    for h in range(SB_HEADS):
        dst_ref[pl.ds(h, n, stride=SB_HEADS), :] = r[:, h * dh:(h + 1) * dh]


def _in_proj_conv_kernel(x_ref, w_ref, prev_ref, cw_ref, cb_ref, g_ref, b_ref,
                         y_ref, st_ref, q_ref, k_ref, v_ref, kb_ref, vb_ref,
                         tail_ref, hp_ref, sh_ref, *, seg, steps_per_seq):
    ctx, sub, rc = CONV_CTX_ROWS, CONV_SUBLANES, CONV_ROW_CHUNK
    pad = ctx - CONV_STATE
    tm = x_ref.shape[0]
    c = y_ref.shape[1]
    nseg = tm // seg
    stride = ctx + seg

    def load_cached_context():
        for s in range(nseg):
            tail_ref[s * ctx:s * ctx + pad, :] = jnp.zeros((pad, c), F32)
            tail_ref[s * ctx + pad:(s + 1) * ctx, :] = prev_ref[s]

    if steps_per_seq == 1:
        load_cached_context()
    else:
        pl.when(pl.program_id(0) % steps_per_seq == 0)(load_cached_context)

    xb = x_ref[...].astype(BF16)
    group = lambda j: _dot(xb, w_ref[:, j * c:(j + 1) * c])
    h = group(0) * jax.nn.sigmoid(group(1))
    for s in range(nseg):
        hp_ref[s * stride:s * stride + ctx, :] = tail_ref[s * ctx:(s + 1) * ctx, :]
        hp_ref[s * stride + ctx:(s + 1) * stride, :] = h[s * seg:(s + 1) * seg, :]

    dh = k_ref.shape[1]

    def qkv_tile(j, l0):
        lanes = slice(l0, l0 + MXU_ROWS)
        r = _dot(xb, w_ref[:, j * c + l0:j * c + l0 + MXU_ROWS])
        if j == 2:
            q_ref[:, lanes] = r.astype(BF16)
            return
        state_ref, bf_ref = (k_ref, kb_ref) if j == 3 else (v_ref, vb_ref)
        for i in range(MXU_ROWS // dh):
            state_ref[pl.ds(l0 // dh + i, tm, stride=SB_HEADS), :] = r[:, i * dh:(i + 1) * dh]
        bf_ref[:, lanes] = r.astype(BF16)

    qkv_tiles = [(j, l0) for j in (2, 3, 4) for l0 in range(0, c, MXU_ROWS)]

    span = nseg * stride - sub
    for r in range(1, sub):
        sh_ref[r - 1, 0:span, :] = hp_ref[r:r + span, :]

    for tile in qkv_tiles:
        qkv_tile(*tile)

    groups = rc // sub
    for s in range(nseg):
        for ci in range(seg // rc):
            r0 = s * stride + ci * rc
            acc = [jnp.broadcast_to(cb_ref[...], (sub, c)) for _ in range(groups)]
            for j in range(CONV_WIDTH):
                a, r = divmod(pad + j, sub)
                wj = cw_ref[j]
                for gi in range(groups):
                    lo = r0 + (a + gi) * sub
                    rows = hp_ref[lo:lo + sub, :] if r == 0 else sh_ref[r - 1, lo:lo + sub, :]
                    acc[gi] = acc[gi] + wj * rows
            y = _layer_norm(jnp.concatenate(acc, axis=0), g_ref[...], b_ref[...])
            y_ref[s * seg + ci * rc:s * seg + (ci + 1) * rc, :] = (y * jax.nn.sigmoid(y)).astype(y_ref.dtype)

    for s in range(nseg):
        last = hp_ref[s * stride + seg:(s + 1) * stride, :]
        tail_ref[s * ctx:(s + 1) * ctx, :] = last
        st_ref[s] = last[pad:, :]


def _in_proj_conv(x, w_in, prev, conv_w, conv_b, ln_g, ln_b, *, tm, seg, steps_per_seq):
    n, d = x.shape
    cols = w_in.shape[1]
    c = cols // 5
    dh = c // SB_HEADS
    nseg = tm // seg
    nreq = prev.shape[0]
    hp_rows = nseg * (CONV_CTX_ROWS + seg)
    out_block = pl.BlockSpec((tm, c), lambda i: (i, 0))
    state_block = pl.BlockSpec((tm * SB_HEADS, dh), lambda i: (i, 0))
    conv_state_block = pl.BlockSpec((nseg, CONV_STATE, c), lambda i: (i // steps_per_seq, 0, 0))
    vec = pl.BlockSpec((1, c), lambda i: (0, 0))
    return pl.pallas_call(
        functools.partial(_in_proj_conv_kernel, seg=seg, steps_per_seq=steps_per_seq),
        out_shape=(
            jax.ShapeDtypeStruct((n, c), BF16),
            jax.ShapeDtypeStruct((nreq, CONV_STATE, c), F32),
            jax.ShapeDtypeStruct((n, c), BF16),
            jax.ShapeDtypeStruct((n * SB_HEADS, dh), F32),
            jax.ShapeDtypeStruct((n * SB_HEADS, dh), F32),
            jax.ShapeDtypeStruct((n, c), BF16),
            jax.ShapeDtypeStruct((n, c), BF16),
        ),
        grid=(n // tm,),
        in_specs=[
            pl.BlockSpec((tm, d), lambda i: (i, 0)),
            pl.BlockSpec((d, cols), lambda i: (0, 0), pipeline_mode=pl.Buffered(1)),
            conv_state_block,
            pl.BlockSpec((CONV_WIDTH, CONV_SUBLANES, c), lambda i: (0, 0, 0)),
            vec, vec, vec,
        ],
        out_specs=(out_block, conv_state_block, out_block, state_block, state_block, out_block, out_block),
        scratch_shapes=[pltpu.VMEM((nseg * CONV_CTX_ROWS, c), F32),
                        pltpu.VMEM((hp_rows, c), F32),
                        pltpu.VMEM((CONV_SUBLANES - 1, hp_rows - CONV_SUBLANES, c), F32)],
        compiler_params=_params("arbitrary"),
        name="in_proj_conv",
    )(x, w_in, prev, jnp.broadcast_to(conv_w[:, None, :], (CONV_WIDTH, CONV_SUBLANES, c)), conv_b, ln_g, ln_b)


def _sb_blocks(qs, ks, vs, tri, state, mask, scale):
    n = len(qs)
    tq = qs[0].shape[0]
    zs = [_dot_nt(qs[a], ks[a]) * scale for a in range(n)]
    sps = []
    for z in zs:
        sp = jnp.maximum(z, 0.0) + jnp.log(1.0 + jnp.exp(-jnp.abs(z)))
        sps.append(sp if mask is None else jnp.where(mask, sp, 0.0))
    parts = []
    for sp in sps:
        hi = sp.astype(BF16)
        parts += [hi, (sp - hi.astype(F32)).astype(BF16)]
    sums = _dot(jnp.concatenate(parts, axis=0), tri)
    out = []
    for a in range(n):
        carry, acc = state[a]
        later = sums[2 * a * tq:(2 * a + 1) * tq] + sums[(2 * a + 1) * tq:(2 * a + 2) * tq]
        w = jnp.exp(zs[a] - sps[a] - later - carry)
        if mask is not None:
            w = jnp.where(mask, w, 0.0)
        out.append((carry + jnp.sum(sps[a], axis=1, keepdims=True), acc + _dot(w.astype(BF16), vs[a])))
    return tuple(out)


def _causal_mask(n):
    row = lax.broadcasted_iota(jnp.int32, (n, n), 0)
    col = lax.broadcasted_iota(jnp.int32, (n, n), 1)
    return col < row


def _sb_prompt_kernel(q_ref, k_ref, v_ref, tri_ref, o_ref, *, tq, dh, scale):
    qi = pl.program_id(2)
    heads = q_ref.shape[1] // dh
    tri = tri_ref[...]
    cols = [slice(a * dh, (a + 1) * dh) for a in range(heads)]
    qs = [q_ref[:, c] for c in cols]

    def visit(s0, state, mask):
        ks = [k_ref[0, pl.ds(s0, tq), c] for c in cols]
        vs = [v_ref[0, pl.ds(s0, tq), c] for c in cols]
        return _sb_blocks(qs, ks, vs, tri, state, mask, scale)

    state = tuple((jnp.zeros((tq, 1), F32), jnp.zeros((tq, dh), F32)) for _ in range(heads))
    state = visit(pl.multiple_of(qi * tq, tq), state, _causal_mask(tq))

    def live(st):
        lowest = functools.reduce(jnp.minimum, [jnp.min(c) for c, _ in st])
        return lowest < SB_CARRY_CUTOFF

    def cond(loop):
        i, more, _ = loop
        return jnp.logical_and(i < qi, more)

    def body(loop):
        i, _, st = loop
        st = visit(pl.multiple_of((qi - 1 - i) * tq, tq), st, None)
        return i + 1, live(st), st

    state = lax.while_loop(cond, body, (jnp.int32(0), live(state), state))[2]
    for a in range(heads):
        o_ref[:, cols[a]] = state[a][1].astype(o_ref.dtype)


def _sb_attention_prompt(q, kb, vb, tri, *, bsz, t, tq, heads_per_step):
    n, c = q.shape
    dh = c // SB_HEADS
    nq = t // tq
    w = heads_per_step * dh
    kv_spec = pl.BlockSpec((1, t, w), lambda b, g, i: (b, 0, g))
    return pl.pallas_call(
        functools.partial(_sb_prompt_kernel, tq=tq, dh=dh, scale=dh ** -0.5),
        out_shape=jax.ShapeDtypeStruct((n, c), BF16),
        grid=(bsz, SB_HEADS // heads_per_step, nq),
        in_specs=[
            pl.BlockSpec((tq, w), lambda b, g, i: (b * nq + i, g)),
            kv_spec, kv_spec,
            pl.BlockSpec((tq, tq), lambda b, g, i: (0, 0)),
        ],
        out_specs=pl.BlockSpec((tq, w), lambda b, g, i: (b * nq + i, g)),
        compiler_params=_params("parallel", "parallel", "arbitrary"),
        name="sb_attention_prompt",
    )(q, kb.reshape(bsz, t, c), vb.reshape(bsz, t, c), tri)


def _sb_sample_kernel(q_ref, kn_ref, vn_ref, kp_ref, vp_ref, trin_ref, trip_ref, o_ref, *, tk, dh, scale):
    tq = q_ref.shape[0]
    cols = [slice(a * dh, (a + 1) * dh) for a in range(SB_HEADS)]
    qs = [q_ref[:, c] for c in cols]
    tri_new = trin_ref[...]
    tri_past = trip_ref[...]
    state = tuple((jnp.zeros((tq, 1), F32), jnp.zeros((tq, dh), F32)) for _ in range(SB_HEADS))
    state = _sb_blocks(qs, [kn_ref[:, c] for c in cols], [vn_ref[:, c] for c in cols], tri_new,
                       state, _causal_mask(tq), scale)
    n_past = kp_ref.shape[1] // (tk * SB_HEADS)
    for i in range(n_past):
        s0 = (n_past - 1 - i) * tk * SB_HEADS
        rows = [pl.ds(s0 + a, tk, stride=SB_HEADS) for a in range(SB_HEADS)]
        state = _sb_blocks(qs, [kp_ref[0, r, :].astype(BF16) for r in rows],
                           [vp_ref[0, r, :].astype(BF16) for r in rows], tri_past, state, None, scale)
    for a in range(SB_HEADS):
        o_ref[:, cols[a]] = state[a][1].astype(o_ref.dtype)


def _sb_attention_sample(q, kb_new, vb_new, k_past, v_past, tri_new, tri_past, *, bsz, t, tk):
    n, c = q.shape
    dh = c // SB_HEADS
    rows_past = k_past.shape[1]
    new_spec = pl.BlockSpec((t, c), lambda b: (b, 0))
    past_spec = pl.BlockSpec((1, rows_past, dh), lambda b: (b, 0, 0))
    return pl.pallas_call(
        functools.partial(_sb_sample_kernel, tk=tk, dh=dh, scale=dh ** -0.5),
        out_shape=jax.ShapeDtypeStruct((n, c), BF16),
        grid=(bsz,),
        in_specs=[
            new_spec, new_spec, new_spec, past_spec, past_spec,
            pl.BlockSpec((t, t), lambda b: (0, 0)),
            pl.BlockSpec((tk, tk), lambda b: (0, 0)),
        ],
        out_specs=new_spec,
        compiler_params=_params("parallel"),
        name="sb_attention_sample",
    )(q, kb_new, vb_new, k_past, v_past, tri_new, tri_past)


def _strict_lower_ones(n):
    idx = jnp.arange(n)
    return (idx[:, None] > idx[None, :]).astype(BF16)


def _out_proj_kernel(x_ref, yc_ref, ys_ref, wc_ref, ws_ref, g_ref, b_ref, o_ref, *, alpha):
    for rows in _row_parts(x_ref.shape[0]):
        mix = _dot(yc_ref[rows, :], wc_ref[...]) + _dot(ys_ref[rows, :], ws_ref[...])
        o_ref[rows, :] = _layer_norm(alpha * x_ref[rows, :] + mix, g_ref[...], b_ref[...])


def _out_proj_post_norm(x, y_conv, y_sb, w_out, g, b, *, alpha, tm):
    n, d = x.shape
    c = y_conv.shape[1]
    vec = pl.BlockSpec((1, d), lambda i: (0, 0))
    return pl.pallas_call(
        functools.partial(_out_proj_kernel, alpha=alpha),
        out_shape=jax.ShapeDtypeStruct((n, d), F32),
        grid=(n // tm,),
        in_specs=[
            pl.BlockSpec((tm, d), lambda i: (i, 0)),
            pl.BlockSpec((tm, c), lambda i: (i, 0)),
            pl.BlockSpec((tm, c), lambda i: (i, 0)),
            pl.BlockSpec((c, d), lambda i: (0, 0)),
            pl.BlockSpec((c, d), lambda i: (1, 0)),
            vec, vec,
        ],
        out_specs=pl.BlockSpec((tm, d), lambda i: (i, 0)),
        compiler_params=_params("parallel"),
        name="out_proj_post_norm",
    )(x, y_conv, y_sb, w_out, w_out, g, b)


def _mem_kv_kernel(m_ref, wk_ref, wv_ref, k_ref, v_ref):
    mb = m_ref[...].astype(BF16)
    k_ref[...] = _dot(mb, wk_ref[...])
    v_ref[...] = _dot(mb, wv_ref[...])


def _memory_kv(mem, wk, wv, *, tm):
    n, d = mem.shape
    c = wk.shape[1]
    w_spec = pl.BlockSpec((d, c), lambda i: (0, 0))
    o_spec = pl.BlockSpec((tm, c), lambda i: (i, 0))
    return pl.pallas_call(
        _mem_kv_kernel,
        out_shape=(jax.ShapeDtypeStruct((n, c), F32), jax.ShapeDtypeStruct((n, c), F32)),
        grid=(n // tm,),
        in_specs=[pl.BlockSpec((tm, d), lambda i: (i, 0)), w_spec, w_spec],
        out_specs=(o_spec, o_spec),
        compiler_params=_params("parallel"),
        name="memory_kv",
    )(mem, wk, wv)


def _xattn_kernel(x_ref, mk_ref, mv_ref, wq_ref, wo_ref, g_ref, b_ref, o_ref, *, alpha, scale):
    mk = mk_ref[0].astype(BF16)
    mv = mv_ref[0].astype(BF16)
    dh = mk.shape[1] // XATTN_HEADS
    for rows in _row_parts(x_ref.shape[0]):
        x = x_ref[rows, :]
        q = _dot(x.astype(BF16), wq_ref[...])
        heads = []
        for h in range(XATTN_HEADS):
            sl = slice(h * dh, (h + 1) * dh)
            s = _dot_nt(q[:, sl].astype(BF16), mk[:, sl]) * scale
            e = jnp.exp(s - jnp.max(s, axis=-1, keepdims=True))
            p = e / jnp.sum(e, axis=-1, keepdims=True)
            heads.append(_dot(p.astype(BF16), mv[:, sl]).astype(BF16))
        o = jnp.concatenate(heads, axis=1)
        y = alpha * x + _dot(o, wo_ref[...])
        o_ref[rows, :] = _layer_norm(y, g_ref[...], b_ref[...])


def _xattn_post_norm(x, mem_k, mem_v, wq, wo, g, b, *, alpha, bsz, t, tm):
    n, d = x.shape
    m, c = mem_k.shape[1], mem_k.shape[2]
    nt = t // tm
    vec = pl.BlockSpec((1, d), lambda bi, i: (0, 0))
    mem_spec = pl.BlockSpec((1, m, c), lambda bi, i: (bi, 0, 0))
    return pl.pallas_call(
        functools.partial(_xattn_kernel, alpha=alpha, scale=(c // XATTN_HEADS) ** -0.5),
        out_shape=jax.ShapeDtypeStruct((n, d), F32),
        grid=(bsz, nt),
        in_specs=[
            pl.BlockSpec((tm, d), lambda bi, i: (bi * nt + i, 0)),
            mem_spec, mem_spec,
            pl.BlockSpec((d, c), lambda bi, i: (0, 0)),
            pl.BlockSpec((c, d), lambda bi, i: (0, 0)),
            vec, vec,
        ],
        out_specs=pl.BlockSpec((tm, d), lambda bi, i: (bi * nt + i, 0)),
        compiler_params=_params("parallel", "parallel"),
        name="xattn_post_norm",
    )(x, mem_k, mem_v, wq, wo, g, b)


LATER_MATRICES = ("w_in", "w_out", "xq", "xo", "f2g", "f2u", "f2d")


def _encoder_layer(x3, conv_prev, sb_past, mem_k, mem_v, w, *, alpha, tm, tq):
    bsz, t, d = x3.shape
    x = x3.reshape(bsz * t, d)
    pending = [name for name in LATER_MATRICES if w[name].dtype != BF16]
    ffn_tm = min(bsz * t, FFN_ROWS)
    x, cast = _ffn_post_norm(x, w["f1g"], w["f1u"], w["f1d"], w["ln1g"], w["ln1b"], alpha=alpha, tm=ffn_tm, tf=512,
                             cast=tuple(w[name] for name in pending))
    w = {**w, **dict(zip(pending, cast))}
    seg = min(t, IN_PROJ_ROWS)
    y_conv, conv_new, q, k, v, kb, vb = _in_proj_conv(
        x, w["w_in"], conv_prev, w["conv_w"], w["conv_b"], w["cln_g"], w["cln_b"],
        tm=IN_PROJ_ROWS, seg=seg, steps_per_seq=t // seg)
    c = y_conv.shape[1]
    if sb_past is None:
        y_sb = _sb_attention_prompt(q, kb, vb, _strict_lower_ones(tq), bsz=bsz, t=t, tq=tq, heads_per_step=8)
    else:
        k_past, v_past = sb_past
        y_sb = _sb_attention_sample(q, kb, vb, k_past, v_past, _strict_lower_ones(t), _strict_lower_ones(tq),
                                    bsz=bsz, t=t, tk=tq)
    x = _out_proj_post_norm(x, y_conv, y_sb, w["w_out"], w["ln2g"], w["ln2b"], alpha=alpha, tm=tm)
    x = _xattn_post_norm(x, mem_k, mem_v, w["xq"], w["xo"], w["ln3g"], w["ln3b"],
                         alpha=alpha, bsz=bsz, t=t, tm=min(tm, t))
    x, _ = _ffn_post_norm(x, w["f2g"], w["f2u"], w["f2d"], w["ln4g"], w["ln4b"], alpha=alpha, tm=ffn_tm, tf=512)
    dh = c // SB_HEADS
    return (x.reshape(bsz, t, d), conv_new, k.reshape(bsz, t, SB_HEADS, dh), v.reshape(bsz, t, SB_HEADS, dh)), w


def kernel(x_prompt, x_sample, mem_prompt, cache_conv, cache_sb_k, cache_sb_v, cache_mem_k, cache_mem_v,
           ffn1_w_gate, ffn1_w_up, ffn1_w_down, ln1_g, ln1_b, w_in, conv_w, conv_b, conv_ln_g, conv_ln_b,
           w_out, ln2_g, ln2_b, xattn_wq, xattn_wk, xattn_wv, xattn_wo, ln3_g, ln3_b,
           ffn2_w_gate, ffn2_w_up, ffn2_w_down, ln4_g, ln4_b):
    depth = ffn1_w_gate.shape[0]
    alpha = (2.0 * depth) ** 0.25
    bp, tp, d = x_prompt.shape
    bs, ts, _ = x_sample.shape
    conv_dim = conv_w.shape[2]
    xattn_dim = xattn_wq.shape[2]

    xp, xs = x_prompt, x_sample
    conv_p, k_p, v_p, mk_p, mv_p, conv_s, k_s, v_s = [], [], [], [], [], [], [], []
    for l in range(depth):
        row = lambda a: a[l][None, :]
        w = dict(
            f1g=ffn1_w_gate[l].astype(BF16), f1u=ffn1_w_up[l].astype(BF16), f1d=ffn1_w_down[l].astype(BF16),
            ln1g=row(ln1_g), ln1b=row(ln1_b), w_in=w_in[l],
            conv_w=conv_w[l], conv_b=row(conv_b), cln_g=row(conv_ln_g), cln_b=row(conv_ln_b),
            w_out=w_out[l], ln2g=row(ln2_g), ln2b=row(ln2_b),
            xq=xattn_wq[l], xo=xattn_wo[l], ln3g=row(ln3_g), ln3b=row(ln3_b),
            f2g=ffn2_w_gate[l], f2u=ffn2_w_up[l], f2d=ffn2_w_down[l],
            ln4g=row(ln4_g), ln4b=row(ln4_b),
        )
        m = mem_prompt.shape[1]
        mk, mv = _memory_kv(mem_prompt.reshape(bp * m, d), xattn_wk[l].astype(BF16), xattn_wv[l].astype(BF16),
                            tm=512)
        mk = mk.reshape(bp, m, xattn_dim)
        mv = mv.reshape(bp, m, xattn_dim)
        (xp, cp, kp, vp), w = _encoder_layer(xp, jnp.zeros((bp, CONV_STATE, conv_dim), F32), None, mk, mv, w,
                                             alpha=alpha, tm=512, tq=256)
        conv_p.append(cp); k_p.append(kp); v_p.append(vp)
        mk_p.append(mk.reshape(bp, m, XATTN_HEADS, xattn_dim // XATTN_HEADS))
        mv_p.append(mv.reshape(bp, m, XATTN_HEADS, xattn_dim // XATTN_HEADS))

        p_len = cache_sb_k.shape[2]
        dh = cache_sb_k.shape[4]
        sb_past = (cache_sb_k[l].reshape(bs, p_len * SB_HEADS, dh), cache_sb_v[l].reshape(bs, p_len * SB_HEADS, dh))
        (xs, cs, ks, vs), _ = _encoder_layer(xs, cache_conv[l], sb_past,
                                             cache_mem_k[l].reshape(bs, m, xattn_dim),
                                             cache_mem_v[l].reshape(bs, m, xattn_dim), w,
                                             alpha=alpha, tm=bs * ts, tq=256)
        conv_s.append(cs); k_s.append(ks); v_s.append(vs)
    stack = lambda parts: parts[0][None] if len(parts) == 1 else jnp.stack(parts)
    return (xp, xs, stack(conv_p), stack(k_p), stack(v_p), stack(mk_p), stack(mv_p),
            stack(conv_s), stack(k_s), stack(v_s))
```

```python
import functools

import jax
import jax.numpy as jnp
from jax import lax
from jax.experimental import pallas as pl
from jax.experimental.pallas import tpu as pltpu

F32 = jnp.float32
BF16 = jnp.bfloat16

LN_EPS = 1e-5
CONV_WIDTH = 31
CONV_STATE = CONV_WIDTH - 1
SB_HEADS = 8
XATTN_HEADS = 4
CONV_CTX_ROWS = 32
CONV_ROW_CHUNK = 16
CONV_SUBLANES = 8
MXU_ROWS = 256
CAST_SLAB_ROWS = 16
IN_PROJ_ROWS = 256
SB_CARRY_CUTOFF = 110.0
VMEM_LIMIT_BYTES = 56 * 1024 * 1024
FFN_VMEM_LIMIT_BYTES = 60 * 1024 * 1024
FFN_ROWS = 1024


def _layer_norm(y, g, b):
    mu = jnp.mean(y, axis=-1, keepdims=True)
    d = y - mu
    var = jnp.mean(d * d, axis=-1, keepdims=True)
    return d * lax.rsqrt(var + LN_EPS) * g + b


def _dot(a, b):
    return jnp.dot(a, b, preferred_element_type=F32)


def _dot_nt(a, b):
    return lax.dot_general(a, b, (((1,), (1,)), ((), ())), preferred_element_type=F32)


def _row_parts(tm):
    parts = 2 if tm % (2 * MXU_ROWS) == 0 else 1
    return [slice(r * tm // parts, (r + 1) * tm // parts) for r in range(parts)]


def _params(*sem, flags=None):
    return pltpu.CompilerParams(dimension_semantics=sem, vmem_limit_bytes=VMEM_LIMIT_BYTES, flags=flags)


def _ffn_kernel(x_ref, wg_ref, wu_ref, wd_ref, g_ref, b_ref, *rest, alpha, nf, n_cast):
    cast_in, o_ref, cast_out, acc_ref = rest[:n_cast], rest[n_cast], rest[n_cast + 1:-1], rest[-1]
    f = pl.program_id(1)

    def step(first, last):
        for src, dst in zip(cast_in, cast_out):
            dst[...] = src[...].astype(BF16)
        tm = x_ref.shape[0]
        parts = max(1, tm // MXU_ROWS) if last else 1
        for r in range(parts):
            rows = slice(r * tm // parts, (r + 1) * tm // parts)
            x = x_ref[rows, :]
            xb = x.astype(BF16)
            gate = _dot(xb, wg_ref[...])
            up = _dot(xb, wu_ref[...])
            h = (gate * jax.nn.sigmoid(gate) * up).astype(BF16)
            total = _dot(h, wd_ref[...])
            if not first:
                total = acc_ref[rows, :] + total
            if last:
                o_ref[rows, :] = _layer_norm(alpha * x + 0.5 * total, g_ref[...], b_ref[...])
            else:
                acc_ref[rows, :] = total

    if nf == 1:
        step(True, True)
    else:
        pl.when(f == 0)(lambda: step(True, False))
        pl.when(jnp.logical_and(f > 0, f < nf - 1))(lambda: step(False, False))
        pl.when(f == nf - 1)(lambda: step(False, True))


def _cast_slab_spec(rows, cols, n_tiles, nf):
    slab = CAST_SLAB_ROWS
    while rows // slab // n_tiles > nf:
        slab += CAST_SLAB_ROWS
    per_tile = rows // slab // n_tiles
    assert per_tile * slab * n_tiles == rows and per_tile >= 1, (rows, n_tiles, nf)
    return pl.BlockSpec((slab, cols), lambda i, f: (i * per_tile + jnp.minimum(f, per_tile - 1), 0))


def _ffn_post_norm(x, wg, wu, wd, g, b, *, alpha, tm, tf, cast=()):
    n, d = x.shape
    dff = wg.shape[1]
    nf = dff // tf
    cast_specs = [_cast_slab_spec(w.shape[0], w.shape[1], n // tm, nf) for w in cast]
    out = pl.pallas_call(
        functools.partial(_ffn_kernel, alpha=alpha, nf=nf, n_cast=len(cast)),
        out_shape=(jax.ShapeDtypeStruct((n, d), F32), *[jax.ShapeDtypeStruct(w.shape, BF16) for w in cast]),
        grid=(n // tm, nf),
        in_specs=[
            pl.BlockSpec((tm, d), lambda i, f: (i, 0)),
            pl.BlockSpec((d, tf), lambda i, f: (0, f)),
            pl.BlockSpec((d, tf), lambda i, f: (0, f)),
            pl.BlockSpec((tf, d), lambda i, f: (f, 0)),
            pl.BlockSpec((1, d), lambda i, f: (0, 0)),
            pl.BlockSpec((1, d), lambda i, f: (0, 0)),
            *cast_specs,
        ],
        out_specs=(pl.BlockSpec((tm, d), lambda i, f: (i, 0), pipeline_mode=pl.Buffered(1)), *cast_specs),
        scratch_shapes=[pltpu.VMEM((tm, d), F32)],
        compiler_params=pltpu.CompilerParams(dimension_semantics=("arbitrary", "arbitrary"),
                                             vmem_limit_bytes=FFN_VMEM_LIMIT_BYTES),
        name="ffn_post_norm",
    )(x, wg, wu, wd, g, b, *cast)
    return out[0], out[1:]


def _store_heads_interleaved(dst_ref, r):
    n, c = r.shape
    dh = c // SB_HEADS
    for h in range(SB_HEADS):
        dst_ref[pl.ds(h, n, stride=SB_HEADS), :] = r[:, h * dh:(h + 1) * dh]


def _in_proj_conv_kernel(x_ref, w_ref, prev_ref, cw_ref, cb_ref, g_ref, b_ref,
                         y_ref, st_ref, q_ref, k_ref, v_ref, kb_ref, vb_ref,
                         tail_ref, hp_ref, sh_ref, *, seg, steps_per_seq):
    ctx, sub, rc = CONV_CTX_ROWS, CONV_SUBLANES, CONV_ROW_CHUNK
    pad = ctx - CONV_STATE
    tm = x_ref.shape[0]
    c = y_ref.shape[1]
    nseg = tm // seg
    stride = ctx + seg

    def load_cached_context():
        for s in range(nseg):
            tail_ref[s * ctx:s * ctx + pad, :] = jnp.zeros((pad, c), F32)
            tail_ref[s * ctx + pad:(s + 1) * ctx, :] = prev_ref[s]

    if steps_per_seq == 1:
        load_cached_context()
    else:
        pl.when(pl.program_id(0) % steps_per_seq == 0)(load_cached_context)

    xb = x_ref[...].astype(BF16)
    group = lambda j: _dot(xb, w_ref[:, j * c:(j + 1) * c])
    h = group(0) * jax.nn.sigmoid(group(1))
    for s in range(nseg):
        hp_ref[s * stride:s * stride + ctx, :] = tail_ref[s * ctx:(s + 1) * ctx, :]
        hp_ref[s * stride + ctx:(s + 1) * stride, :] = h[s * seg:(s + 1) * seg, :]

    dh = k_ref.shape[1]

    def qkv_tile(j, l0):
        lanes = slice(l0, l0 + MXU_ROWS)
        r = _dot(xb, w_ref[:, j * c + l0:j * c + l0 + MXU_ROWS])
        if j == 2:
            q_ref[:, lanes] = r.astype(BF16)
        else:
            state_ref, bf_ref = (k_ref, kb_ref) if j == 3 else (v_ref, vb_ref)
            for i in range(MXU_ROWS // dh):
                state_ref[pl.ds(l0 // dh + i, tm, stride=SB_HEADS), :] = r[:, i * dh:(i + 1) * dh]
            bf_ref[:, lanes] = r.astype(BF16)
        return jnp.concatenate([r[0:sub, :]] * (c // MXU_ROWS), axis=1)

    qkv_tiles = [(j, l0) for j in (2, 3, 4) for l0 in range(0, c, MXU_ROWS)]

    span = nseg * stride - sub
    for r in range(1, sub):
        sh_ref[r - 1, 0:span, :] = hp_ref[r:r + span, :]

    never = pl.program_id(0) < 0
    n_chunks = nseg * (seg // rc)
    groups = rc // sub
    previous = None
    for s in range(nseg):
        for ci in range(seg // rc):
            done = s * (seg // rc) + ci
            start = jnp.broadcast_to(cb_ref[...], (sub, c))
            for tile in qkv_tiles[done * len(qkv_tiles) // n_chunks:(done + 1) * len(qkv_tiles) // n_chunks]:
                start = jnp.where(never, qkv_tile(*tile), start)
            if previous is not None:
                start = jnp.where(never, previous, start)
            r0 = s * stride + ci * rc
            acc = [start for _ in range(groups)]
            for j in range(CONV_WIDTH):
                a, r = divmod(pad + j, sub)
                wj = cw_ref[j]
                for gi in range(groups):
                    lo = r0 + (a + gi) * sub
                    rows = hp_ref[lo:lo + sub, :] if r == 0 else sh_ref[r - 1, lo:lo + sub, :]
                    acc[gi] = acc[gi] + wj * rows
            y = _layer_norm(jnp.concatenate(acc, axis=0), g_ref[...], b_ref[...])
            previous = y[0:sub, :]
            y_ref[s * seg + ci * rc:s * seg + (ci + 1) * rc, :] = (y * jax.nn.sigmoid(y)).astype(y_ref.dtype)

    for s in range(nseg):
        last = hp_ref[s * stride + seg:(s + 1) * stride, :]
        tail_ref[s * ctx:(s + 1) * ctx, :] = last
        st_ref[s] = last[pad:, :]


def _in_proj_conv(x, w_in, prev, conv_w, conv_b, ln_g, ln_b, *, tm, seg, steps_per_seq):
    n, d = x.shape
    cols = w_in.shape[1]
    c = cols // 5
    dh = c // SB_HEADS
    nseg = tm // seg
    nreq = prev.shape[0]
    hp_rows = nseg * (CONV_CTX_ROWS + seg)
    out_block = pl.BlockSpec((tm, c), lambda i: (i, 0))
    state_block = pl.BlockSpec((tm * SB_HEADS, dh), lambda i: (i, 0))
    conv_state_block = pl.BlockSpec((nseg, CONV_STATE, c), lambda i: (i // steps_per_seq, 0, 0))
    vec = pl.BlockSpec((1, c), lambda i: (0, 0))
    return pl.pallas_call(
        functools.partial(_in_proj_conv_kernel, seg=seg, steps_per_seq=steps_per_seq),
        out_shape=(
            jax.ShapeDtypeStruct((n, c), BF16),
            jax.ShapeDtypeStruct((nreq, CONV_STATE, c), F32),
            jax.ShapeDtypeStruct((n, c), BF16),
            jax.ShapeDtypeStruct((n * SB_HEADS, dh), F32),
            jax.ShapeDtypeStruct((n * SB_HEADS, dh), F32),
            jax.ShapeDtypeStruct((n, c), BF16),
            jax.ShapeDtypeStruct((n, c), BF16),
        ),
        grid=(n // tm,),
        in_specs=[
            pl.BlockSpec((tm, d), lambda i: (i, 0)),
            pl.BlockSpec((d, cols), lambda i: (0, 0), pipeline_mode=pl.Buffered(1)),
            conv_state_block,
            pl.BlockSpec((CONV_WIDTH, CONV_SUBLANES, c), lambda i: (0, 0, 0)),
            vec, vec, vec,
        ],
        out_specs=(out_block, conv_state_block, out_block, state_block, state_block, out_block, out_block),
        scratch_shapes=[pltpu.VMEM((nseg * CONV_CTX_ROWS, c), F32),
                        pltpu.VMEM((hp_rows, c), F32),
                        pltpu.VMEM((CONV_SUBLANES - 1, hp_rows - CONV_SUBLANES, c), F32)],
        compiler_params=_params("arbitrary"),
        name="in_proj_conv",
    )(x, w_in, prev, jnp.broadcast_to(conv_w[:, None, :], (CONV_WIDTH, CONV_SUBLANES, c)), conv_b, ln_g, ln_b)


def _sb_blocks(qs, ks, vs, tri, state, mask, scale):
    n = len(qs)
    tq = qs[0].shape[0]
    zs = [_dot_nt(qs[a], ks[a]) * scale for a in range(n)]
    sps = []
    for z in zs:
        sp = jnp.maximum(z, 0.0) + jnp.log(1.0 + jnp.exp(-jnp.abs(z)))
        sps.append(sp if mask is None else jnp.where(mask, sp, 0.0))
    parts = []
    for sp in sps:
        hi = sp.astype(BF16)
        parts += [hi, (sp - hi.astype(F32)).astype(BF16)]
    sums = _dot(jnp.concatenate(parts, axis=0), tri)
    out = []
    for a in range(n):
        carry, acc = state[a]
        later = sums[2 * a * tq:(2 * a + 1) * tq] + sums[(2 * a + 1) * tq:(2 * a + 2) * tq]
        w = jnp.exp(zs[a] - sps[a] - later - carry)
        if mask is not None:
            w = jnp.where(mask, w, 0.0)
        out.append((carry + jnp.sum(sps[a], axis=1, keepdims=True), acc + _dot(w.astype(BF16), vs[a])))
    return tuple(out)


def _causal_mask(n):
    row = lax.broadcasted_iota(jnp.int32, (n, n), 0)
    col = lax.broadcasted_iota(jnp.int32, (n, n), 1)
    return col < row


def _sb_prompt_kernel(q_ref, k_ref, v_ref, tri_ref, o_ref, *, tq, dh, scale):
    qi = pl.program_id(2)
    heads = q_ref.shape[1] // dh
    tri = tri_ref[...]
    cols = [slice(a * dh, (a + 1) * dh) for a in range(heads)]
    qs = [q_ref[:, c] for c in cols]

    def visit(s0, state, mask):
        ks = [k_ref[0, pl.ds(s0, tq), c] for c in cols]
        vs = [v_ref[0, pl.ds(s0, tq), c] for c in cols]
        return _sb_blocks(qs, ks, vs, tri, state, mask, scale)

    state = tuple((jnp.zeros((tq, 1), F32), jnp.zeros((tq, dh), F32)) for _ in range(heads))
    state = visit(pl.multiple_of(qi * tq, tq), state, _causal_mask(tq))

    def live(st):
        lowest = functools.reduce(jnp.minimum, [jnp.min(c) for c, _ in st])
        return lowest < SB_CARRY_CUTOFF

    def cond(loop):
        i, more, _ = loop
        return jnp.logical_and(i < qi, more)

    def body(loop):
        i, _, st = loop
        st = visit(pl.multiple_of((qi - 1 - i) * tq, tq), st, None)
        return i + 1, live(st), st

    state = lax.while_loop(cond, body, (jnp.int32(0), live(state), state))[2]
    for a in range(heads):
        o_ref[:, cols[a]] = state[a][1].astype(o_ref.dtype)


def _sb_attention_prompt(q, kb, vb, tri, *, bsz, t, tq, heads_per_step):
    n, c = q.shape
    dh = c // SB_HEADS
    nq = t // tq
    w = heads_per_step * dh
    kv_spec = pl.BlockSpec((1, t, w), lambda b, g, i: (b, 0, g))
    return pl.pallas_call(
        functools.partial(_sb_prompt_kernel, tq=tq, dh=dh, scale=dh ** -0.5),
        out_shape=jax.ShapeDtypeStruct((n, c), BF16),
        grid=(bsz, SB_HEADS // heads_per_step, nq),
        in_specs=[
            pl.BlockSpec((tq, w), lambda b, g, i: (b * nq + i, g)),
            kv_spec, kv_spec,
            pl.BlockSpec((tq, tq), lambda b, g, i: (0, 0)),
        ],
        out_specs=pl.BlockSpec((tq, w), lambda b, g, i: (b * nq + i, g)),
        compiler_params=_params("parallel", "parallel", "arbitrary"),
        name="sb_attention_prompt",
    )(q, kb.reshape(bsz, t, c), vb.reshape(bsz, t, c), tri)


def _sb_sample_kernel(q_ref, kn_ref, vn_ref, kp_ref, vp_ref, trin_ref, trip_ref, o_ref, *, tk, dh, scale):
    tq = q_ref.shape[0]
    cols = [slice(a * dh, (a + 1) * dh) for a in range(SB_HEADS)]
    qs = [q_ref[:, c] for c in cols]
    tri_new = trin_ref[...]
    tri_past = trip_ref[...]
    state = tuple((jnp.zeros((tq, 1), F32), jnp.zeros((tq, dh), F32)) for _ in range(SB_HEADS))
    state = _sb_blocks(qs, [kn_ref[:, c] for c in cols], [vn_ref[:, c] for c in cols], tri_new,
                       state, _causal_mask(tq), scale)
    n_past = kp_ref.shape[1] // (tk * SB_HEADS)
    for i in range(n_past):
        s0 = (n_past - 1 - i) * tk * SB_HEADS
        rows = [pl.ds(s0 + a, tk, stride=SB_HEADS) for a in range(SB_HEADS)]
        state = _sb_blocks(qs, [kp_ref[0, r, :].astype(BF16) for r in rows],
                           [vp_ref[0, r, :].astype(BF16) for r in rows], tri_past, state, None, scale)
    for a in range(SB_HEADS):
        o_ref[:, cols[a]] = state[a][1].astype(o_ref.dtype)


def _sb_attention_sample(q, kb_new, vb_new, k_past, v_past, tri_new, tri_past, *, bsz, t, tk):
    n, c = q.shape
    dh = c // SB_HEADS
    rows_past = k_past.shape[1]
    new_spec = pl.BlockSpec((t, c), lambda b: (b, 0))
    past_spec = pl.BlockSpec((1, rows_past, dh), lambda b: (b, 0, 0))
    return pl.pallas_call(
        functools.partial(_sb_sample_kernel, tk=tk, dh=dh, scale=dh ** -0.5),
        out_shape=jax.ShapeDtypeStruct((n, c), BF16),
        grid=(bsz,),
        in_specs=[
            new_spec, new_spec, new_spec, past_spec, past_spec,
            pl.BlockSpec((t, t), lambda b: (0, 0)),
            pl.BlockSpec((tk, tk), lambda b: (0, 0)),
        ],
        out_specs=new_spec,
        compiler_params=_params("parallel"),
        name="sb_attention_sample",
    )(q, kb_new, vb_new, k_past, v_past, tri_new, tri_past)


def _strict_lower_ones(n):
    idx = jnp.arange(n)
    return (idx[:, None] > idx[None, :]).astype(BF16)


def _out_proj_kernel(x_ref, yc_ref, ys_ref, wc_ref, ws_ref, g_ref, b_ref, o_ref, *, alpha):
    for rows in _row_parts(x_ref.shape[0]):
        mix = _dot(yc_ref[rows, :], wc_ref[...]) + _dot(ys_ref[rows, :], ws_ref[...])
        o_ref[rows, :] = _layer_norm(alpha * x_ref[rows, :] + mix, g_ref[...], b_ref[...])


def _out_proj_post_norm(x, y_conv, y_sb, w_out, g, b, *, alpha, tm):
    n, d = x.shape
    c = y_conv.shape[1]
    vec = pl.BlockSpec((1, d), lambda i: (0, 0))
    return pl.pallas_call(
        functools.partial(_out_proj_kernel, alpha=alpha),
        out_shape=jax.ShapeDtypeStruct((n, d), F32),
        grid=(n // tm,),
        in_specs=[
            pl.BlockSpec((tm, d), lambda i: (i, 0)),
            pl.BlockSpec((tm, c), lambda i: (i, 0)),
            pl.BlockSpec((tm, c), lambda i: (i, 0)),
            pl.BlockSpec((c, d), lambda i: (0, 0)),
            pl.BlockSpec((c, d), lambda i: (1, 0)),
            vec, vec,
        ],
        out_specs=pl.BlockSpec((tm, d), lambda i: (i, 0)),
        compiler_params=_params("parallel"),
        name="out_proj_post_norm",
    )(x, y_conv, y_sb, w_out, w_out, g, b)


def _mem_kv_kernel(m_ref, wk_ref, wv_ref, k_ref, v_ref):
    mb = m_ref[...].astype(BF16)
    k_ref[...] = _dot(mb, wk_ref[...])
    v_ref[...] = _dot(mb, wv_ref[...])


def _memory_kv(mem, wk, wv, *, tm):
    n, d = mem.shape
    c = wk.shape[1]
    w_spec = pl.BlockSpec((d, c), lambda i: (0, 0))
    o_spec = pl.BlockSpec((tm, c), lambda i: (i, 0))
    return pl.pallas_call(
        _mem_kv_kernel,
        out_shape=(jax.ShapeDtypeStruct((n, c), F32), jax.ShapeDtypeStruct((n, c), F32)),
        grid=(n // tm,),
        in_specs=[pl.BlockSpec((tm, d), lambda i: (i, 0)), w_spec, w_spec],
        out_specs=(o_spec, o_spec),
        compiler_params=_params("parallel"),
        name="memory_kv",
    )(mem, wk, wv)


def _xattn_kernel(x_ref, mk_ref, mv_ref, wq_ref, wo_ref, g_ref, b_ref, o_ref, *, alpha, scale):
    mk = mk_ref[0].astype(BF16)
    mv = mv_ref[0].astype(BF16)
    dh = mk.shape[1] // XATTN_HEADS
    for rows in _row_parts(x_ref.shape[0]):
        x = x_ref[rows, :]
        q = _dot(x.astype(BF16), wq_ref[...])
        heads = []
        for h in range(XATTN_HEADS):
            sl = slice(h * dh, (h + 1) * dh)
            s = _dot_nt(q[:, sl].astype(BF16), mk[:, sl]) * scale
            e = jnp.exp(s - jnp.max(s, axis=-1, keepdims=True))
            p = e / jnp.sum(e, axis=-1, keepdims=True)
            heads.append(_dot(p.astype(BF16), mv[:, sl]).astype(BF16))
        o = jnp.concatenate(heads, axis=1)
        y = alpha * x + _dot(o, wo_ref[...])
        o_ref[rows, :] = _layer_norm(y, g_ref[...], b_ref[...])


def _xattn_post_norm(x, mem_k, mem_v, wq, wo, g, b, *, alpha, bsz, t, tm):
    n, d = x.shape
    m, c = mem_k.shape[1], mem_k.shape[2]
    nt = t // tm
    vec = pl.BlockSpec((1, d), lambda bi, i: (0, 0))
    mem_spec = pl.BlockSpec((1, m, c), lambda bi, i: (bi, 0, 0))
    return pl.pallas_call(
        functools.partial(_xattn_kernel, alpha=alpha, scale=(c // XATTN_HEADS) ** -0.5),
        out_shape=jax.ShapeDtypeStruct((n, d), F32),
        grid=(bsz, nt),
        in_specs=[
            pl.BlockSpec((tm, d), lambda bi, i: (bi * nt + i, 0)),
            mem_spec, mem_spec,
            pl.BlockSpec((d, c), lambda bi, i: (0, 0)),
            pl.BlockSpec((c, d), lambda bi, i: (0, 0)),
            vec, vec,
        ],
        out_specs=pl.BlockSpec((tm, d), lambda bi, i: (bi * nt + i, 0)),
        compiler_params=_params("parallel", "parallel"),
        name="xattn_post_norm",
    )(x, mem_k, mem_v, wq, wo, g, b)


LATER_MATRICES = ("w_in", "w_out", "xq", "xo", "f2g", "f2u", "f2d")


def _encoder_layer(x3, conv_prev, sb_past, mem_k, mem_v, w, *, alpha, tm, tq):
    bsz, t, d = x3.shape
    x = x3.reshape(bsz * t, d)
    pending = [name for name in LATER_MATRICES if w[name].dtype != BF16]
    ffn_tm = min(bsz * t, FFN_ROWS)
    x, cast = _ffn_post_norm(x, w["f1g"], w["f1u"], w["f1d"], w["ln1g"], w["ln1b"], alpha=alpha, tm=ffn_tm, tf=512,
                             cast=tuple(w[name] for name in pending))
    w = {**w, **dict(zip(pending, cast))}
    seg = min(t, IN_PROJ_ROWS)
    y_conv, conv_new, q, k, v, kb, vb = _in_proj_conv(
        x, w["w_in"], conv_prev, w["conv_w"], w["conv_b"], w["cln_g"], w["cln_b"],
        tm=IN_PROJ_ROWS, seg=seg, steps_per_seq=t // seg)
    c = y_conv.shape[1]
    if sb_past is None:
        y_sb = _sb_attention_prompt(q, kb, vb, _strict_lower_ones(tq), bsz=bsz, t=t, tq=tq, heads_per_step=8)
    else:
        k_past, v_past = sb_past
        y_sb = _sb_attention_sample(q, kb, vb, k_past, v_past, _strict_lower_ones(t), _strict_lower_ones(tq),
                                    bsz=bsz, t=t, tk=tq)
    x = _out_proj_post_norm(x, y_conv, y_sb, w["w_out"], w["ln2g"], w["ln2b"], alpha=alpha, tm=tm)
    x = _xattn_post_norm(x, mem_k, mem_v, w["xq"], w["xo"], w["ln3g"], w["ln3b"],
                         alpha=alpha, bsz=bsz, t=t, tm=min(tm, t))
    x, _ = _ffn_post_norm(x, w["f2g"], w["f2u"], w["f2d"], w["ln4g"], w["ln4b"], alpha=alpha, tm=ffn_tm, tf=512)
    dh = c // SB_HEADS
    return (x.reshape(bsz, t, d), conv_new, k.reshape(bsz, t, SB_HEADS, dh), v.reshape(bsz, t, SB_HEADS, dh)), w


def kernel(x_prompt, x_sample, mem_prompt, cache_conv, cache_sb_k, cache_sb_v, cache_mem_k, cache_mem_v,
           ffn1_w_gate, ffn1_w_up, ffn1_w_down, ln1_g, ln1_b, w_in, conv_w, conv_b, conv_ln_g, conv_ln_b,
           w_out, ln2_g, ln2_b, xattn_wq, xattn_wk, xattn_wv, xattn_wo, ln3_g, ln3_b,
           ffn2_w_gate, ffn2_w_up, ffn2_w_down, ln4_g, ln4_b):
    depth = ffn1_w_gate.shape[0]
    alpha = (2.0 * depth) ** 0.25
    bp, tp, d = x_prompt.shape
    bs, ts, _ = x_sample.shape
    conv_dim = conv_w.shape[2]
    xattn_dim = xattn_wq.shape[2]

    xp, xs = x_prompt, x_sample
    conv_p, k_p, v_p, mk_p, mv_p, conv_s, k_s, v_s = [], [], [], [], [], [], [], []
    for l in range(depth):
        row = lambda a: a[l][None, :]
        w = dict(
            f1g=ffn1_w_gate[l].astype(BF16), f1u=ffn1_w_up[l].astype(BF16), f1d=ffn1_w_down[l].astype(BF16),
            ln1g=row(ln1_g), ln1b=row(ln1_b), w_in=w_in[l],
            conv_w=conv_w[l], conv_b=row(conv_b), cln_g=row(conv_ln_g), cln_b=row(conv_ln_b),
            w_out=w_out[l], ln2g=row(ln2_g), ln2b=row(ln2_b),
            xq=xattn_wq[l], xo=xattn_wo[l], ln3g=row(ln3_g), ln3b=row(ln3_b),
            f2g=ffn2_w_gate[l], f2u=ffn2_w_up[l], f2d=ffn2_w_down[l],
            ln4g=row(ln4_g), ln4b=row(ln4_b),
        )
        m = mem_prompt.shape[1]
        mk, mv = _memory_kv(mem_prompt.reshape(bp * m, d), xattn_wk[l].astype(BF16), xattn_wv[l].astype(BF16),
                            tm=512)
        mk = mk.reshape(bp, m, xattn_dim)
        mv = mv.reshape(bp, m, xattn_dim)
        (xp, cp, kp, vp), w = _encoder_layer(xp, jnp.zeros((bp, CONV_STATE, conv_dim), F32), None, mk, mv, w,
                                             alpha=alpha, tm=512, tq=256)
        conv_p.append(cp); k_p.append(kp); v_p.append(vp)
        mk_p.append(mk.reshape(bp, m, XATTN_HEADS, xattn_dim // XATTN_HEADS))
        mv_p.append(mv.reshape(bp, m, XATTN_HEADS, xattn_dim // XATTN_HEADS))

        p_len = cache_sb_k.shape[2]
        dh = cache_sb_k.shape[4]
        sb_past = (cache_sb_k[l].reshape(bs, p_len * SB_HEADS, dh), cache_sb_v[l].reshape(bs, p_len * SB_HEADS, dh))
        (xs, cs, ks, vs), _ = _encoder_layer(xs, cache_conv[l], sb_past,
                                             cache_mem_k[l].reshape(bs, m, xattn_dim),
                                             cache_mem_v[l].reshape(bs, m, xattn_dim), w,
                                             alpha=alpha, tm=bs * ts, tq=256)
        conv_s.append(cs); k_s.append(ks); v_s.append(vs)
    stack = lambda parts: parts[0][None] if len(parts) == 1 else jnp.stack(parts)
    return (xp, xs, stack(conv_p), stack(k_p), stack(v_p), stack(mk_p), stack(mv_p),
            stack(conv_s), stack(k_s), stack(v_s))
```

```python
import functools

import jax
import jax.numpy as jnp
from jax import lax
from jax.experimental import pallas as pl
from jax.experimental.pallas import tpu as pltpu

F32 = jnp.float32
BF16 = jnp.bfloat16

LN_EPS = 1e-5
CONV_WIDTH = 31
CONV_STATE = CONV_WIDTH - 1
SB_HEADS = 8
XATTN_HEADS = 4
CONV_CTX_ROWS = 32
CONV_ROW_CHUNK = 32
CONV_SUBLANES = 8
MXU_ROWS = 256
CAST_SLAB_ROWS = 16
IN_PROJ_ROWS = 256
SB_CARRY_CUTOFF = 110.0
VMEM_LIMIT_BYTES = 56 * 1024 * 1024
FFN_VMEM_LIMIT_BYTES = 60 * 1024 * 1024
FFN_ROWS = 1024
FFN_HIDDEN_TILE = 512
TOKEN_ROWS = 512
SB_BLOCK = 256
SB_HEADS_PER_STEP = 8


def _layer_norm(y, g, b):
    mu = jnp.mean(y, axis=-1, keepdims=True)
    d = y - mu
    var = jnp.mean(d * d, axis=-1, keepdims=True)
    return d * lax.rsqrt(var + LN_EPS) * g + b


def _dot(a, b):
    return jnp.dot(a, b, preferred_element_type=F32)


def _dot_nt(a, b):
    return lax.dot_general(a, b, (((1,), (1,)), ((), ())), preferred_element_type=F32)


def _row_parts(tm):
    parts = 2 if tm % (2 * MXU_ROWS) == 0 else 1
    return [slice(r * tm // parts, (r + 1) * tm // parts) for r in range(parts)]


def _params(*sem, flags=None):
    return pltpu.CompilerParams(dimension_semantics=sem, vmem_limit_bytes=VMEM_LIMIT_BYTES, flags=flags)


def _ffn_kernel(x_ref, wg_ref, wu_ref, wd_ref, g_ref, b_ref, *rest, alpha, nf, n_cast):
    cast_in, o_ref, cast_out, acc_ref = rest[:n_cast], rest[n_cast], rest[n_cast + 1:-1], rest[-1]
    f = pl.program_id(1)

    def step(first, last):
        for src, dst in zip(cast_in, cast_out):
            dst[...] = src[...].astype(BF16)
        tm = x_ref.shape[0]
        parts = max(1, tm // MXU_ROWS) if last else 1
        for r in range(parts):
            rows = slice(r * tm // parts, (r + 1) * tm // parts)
            x = x_ref[rows, :]
            xb = x.astype(BF16)
            gate = _dot(xb, wg_ref[...])
            up = _dot(xb, wu_ref[...])
            h = (gate * jax.nn.sigmoid(gate) * up).astype(BF16)
            total = _dot(h, wd_ref[...])
            if not first:
                total = acc_ref[rows, :] + total
            if last:
                o_ref[rows, :] = _layer_norm(alpha * x + 0.5 * total, g_ref[...], b_ref[...])
            else:
                acc_ref[rows, :] = total

    if nf == 1:
        step(True, True)
    else:
        pl.when(f == 0)(lambda: step(True, False))
        pl.when(jnp.logical_and(f > 0, f < nf - 1))(lambda: step(False, False))
        pl.when(f == nf - 1)(lambda: step(False, True))


def _cast_slab_spec(rows, cols, n_tiles, nf):
    slab = CAST_SLAB_ROWS
    while rows // slab // n_tiles > nf:
        slab += CAST_SLAB_ROWS
    per_tile = rows // slab // n_tiles
    assert per_tile * slab * n_tiles == rows and per_tile >= 1, (rows, n_tiles, nf)
    return pl.BlockSpec((slab, cols), lambda i, f: (i * per_tile + jnp.minimum(f, per_tile - 1), 0))


def _ffn_post_norm(x, wg, wu, wd, g, b, *, alpha, tm, tf, cast=()):
    n, d = x.shape
    dff = wg.shape[1]
    nf = dff // tf
    cast_specs = [_cast_slab_spec(w.shape[0], w.shape[1], n // tm, nf) for w in cast]
    out = pl.pallas_call(
        functools.partial(_ffn_kernel, alpha=alpha, nf=nf, n_cast=len(cast)),
        out_shape=(jax.ShapeDtypeStruct((n, d), F32), *[jax.ShapeDtypeStruct(w.shape, BF16) for w in cast]),
        grid=(n // tm, nf),
        in_specs=[
            pl.BlockSpec((tm, d), lambda i, f: (i, 0)),
            pl.BlockSpec((d, tf), lambda i, f: (0, f)),
            pl.BlockSpec((d, tf), lambda i, f: (0, f)),
            pl.BlockSpec((tf, d), lambda i, f: (f, 0)),
            pl.BlockSpec((1, d), lambda i, f: (0, 0)),
            pl.BlockSpec((1, d), lambda i, f: (0, 0)),
            *cast_specs,
        ],
        out_specs=(pl.BlockSpec((tm, d), lambda i, f: (i, 0), pipeline_mode=pl.Buffered(1)), *cast_specs),
        scratch_shapes=[pltpu.VMEM((tm, d), F32)],
        compiler_params=pltpu.CompilerParams(dimension_semantics=("arbitrary", "arbitrary"),
                                             vmem_limit_bytes=FFN_VMEM_LIMIT_BYTES),
        name="ffn_post_norm",
    )(x, wg, wu, wd, g, b, *cast)
    return out[0], out[1:]


def _store_heads_interleaved(dst_ref, r):
    n, c = r.shape
    dh = c // SB_HEADS
    for h in range(SB_HEADS):
        dst_ref[pl.ds(h, n, stride=SB_HEADS), :] = r[:, h * dh:(h + 1) * dh]


def _in_proj_conv_kernel(x_ref, w_ref, prev_ref, cw_ref, cb_ref, g_ref, b_ref,
                         y_ref, st_ref, q_ref, k_ref, v_ref, kb_ref, vb_ref,
                         tail_ref, hp_ref, sh_ref, *, seg, steps_per_seq):
    ctx, sub, rc = CONV_CTX_ROWS, CONV_SUBLANES, CONV_ROW_CHUNK
    pad = ctx - CONV_STATE
    tm = x_ref.shape[0]
    c = y_ref.shape[1]
    nseg = tm // seg
    stride = ctx + seg

    def load_cached_context():
        for s in range(nseg):
            tail_ref[s * ctx:s * ctx + pad, :] = jnp.zeros((pad, c), F32)
            tail_ref[s * ctx + pad:(s + 1) * ctx, :] = prev_ref[s]

    if steps_per_seq == 1:
        load_cached_context()
    else:
        pl.when(pl.program_id(0) % steps_per_seq == 0)(load_cached_context)

    xb = x_ref[...].astype(BF16)
    group = lambda j: _dot(xb, w_ref[:, j * c:(j + 1) * c])
    h = group(0) * jax.nn.sigmoid(group(1))
    for s in range(nseg):
        hp_ref[s * stride:s * stride + ctx, :] = tail_ref[s * ctx:(s + 1) * ctx, :]
        hp_ref[s * stride + ctx:(s + 1) * stride, :] = h[s * seg:(s + 1) * seg, :]

    dh = k_ref.shape[1]

    def qkv_tile(j, l0):
        lanes = slice(l0, l0 + MXU_ROWS)
        r = _dot(xb, w_ref[:, j * c + l0:j * c + l0 + MXU_ROWS])
        if j == 2:
            q_ref[:, lanes] = r.astype(BF16)
            return
        state_ref, bf_ref = (k_ref, kb_ref) if j == 3 else (v_ref, vb_ref)
        for i in range(MXU_ROWS // dh):
            state_ref[pl.ds(l0 // dh + i, tm, stride=SB_HEADS), :] = r[:, i * dh:(i + 1) * dh]
        bf_ref[:, lanes] = r.astype(BF16)

    qkv_tiles = [(j, l0) for j in (2, 3, 4) for l0 in range(0, c, MXU_ROWS)]

    span = nseg * stride - sub
    for r in range(1, sub):
        sh_ref[r - 1, 0:span, :] = hp_ref[r:r + span, :]

    for tile in qkv_tiles:
        qkv_tile(*tile)

    groups = rc // sub
    for s in range(nseg):
        for ci in range(seg // rc):
            r0 = s * stride + ci * rc
            acc = [jnp.broadcast_to(cb_ref[...], (sub, c)) for _ in range(groups)]
            for j in range(CONV_WIDTH):
                a, r = divmod(pad + j, sub)
                wj = cw_ref[j]
                for gi in range(groups):
                    lo = r0 + (a + gi) * sub
                    rows = hp_ref[lo:lo + sub, :] if r == 0 else sh_ref[r - 1, lo:lo + sub, :]
                    acc[gi] = acc[gi] + wj * rows
            y = _layer_norm(jnp.concatenate(acc, axis=0), g_ref[...], b_ref[...])
            y_ref[s * seg + ci * rc:s * seg + (ci + 1) * rc, :] = (y * jax.nn.sigmoid(y)).astype(y_ref.dtype)

    for s in range(nseg):
        last = hp_ref[s * stride + seg:(s + 1) * stride, :]
        tail_ref[s * ctx:(s + 1) * ctx, :] = last
        st_ref[s] = last[pad:, :]


def _in_proj_conv(x, w_in, prev, conv_w, conv_b, ln_g, ln_b, *, tm, seg, steps_per_seq):
    n, d = x.shape
    cols = w_in.shape[1]
    c = cols // 5
    dh = c // SB_HEADS
    nseg = tm // seg
    nreq = prev.shape[0]
    hp_rows = nseg * (CONV_CTX_ROWS + seg)
    out_block = pl.BlockSpec((tm, c), lambda i: (i, 0))
    state_block = pl.BlockSpec((tm * SB_HEADS, dh), lambda i: (i, 0))
    conv_state_block = pl.BlockSpec((nseg, CONV_STATE, c), lambda i: (i // steps_per_seq, 0, 0))
    vec = pl.BlockSpec((1, c), lambda i: (0, 0))
    return pl.pallas_call(
        functools.partial(_in_proj_conv_kernel, seg=seg, steps_per_seq=steps_per_seq),
        out_shape=(
            jax.ShapeDtypeStruct((n, c), BF16),
            jax.ShapeDtypeStruct((nreq, CONV_STATE, c), F32),
            jax.ShapeDtypeStruct((n, c), BF16),
            jax.ShapeDtypeStruct((n * SB_HEADS, dh), F32),
            jax.ShapeDtypeStruct((n * SB_HEADS, dh), F32),
            jax.ShapeDtypeStruct((n, c), BF16),
            jax.ShapeDtypeStruct((n, c), BF16),
        ),
        grid=(n // tm,),
        in_specs=[
            pl.BlockSpec((tm, d), lambda i: (i, 0)),
            pl.BlockSpec((d, cols), lambda i: (0, 0), pipeline_mode=pl.Buffered(1)),
            conv_state_block,
            pl.BlockSpec((CONV_WIDTH, CONV_SUBLANES, c), lambda i: (0, 0, 0)),
            vec, vec, vec,
        ],
        out_specs=(out_block, conv_state_block, out_block, state_block, state_block, out_block, out_block),
        scratch_shapes=[pltpu.VMEM((nseg * CONV_CTX_ROWS, c), F32),
                        pltpu.VMEM((hp_rows, c), F32),
                        pltpu.VMEM((CONV_SUBLANES - 1, hp_rows - CONV_SUBLANES, c), F32)],
        compiler_params=_params("arbitrary"),
        name="in_proj_conv",
    )(x, w_in, prev, jnp.broadcast_to(conv_w[:, None, :], (CONV_WIDTH, CONV_SUBLANES, c)), conv_b, ln_g, ln_b)


def _sb_blocks(qs, ks, vs, tri, state, mask, scale):
    n = len(qs)
    tq = qs[0].shape[0]
    zs = [_dot_nt(qs[a], ks[a]) * scale for a in range(n)]
    sps = []
    for z in zs:
        sp = jnp.maximum(z, 0.0) + jnp.log(1.0 + jnp.exp(-jnp.abs(z)))
        sps.append(sp if mask is None else jnp.where(mask, sp, 0.0))
    parts = []
    for sp in sps:
        hi = sp.astype(BF16)
        parts += [hi, (sp - hi.astype(F32)).astype(BF16)]
    sums = _dot(jnp.concatenate(parts, axis=0), tri)
    out = []
    for a in range(n):
        carry, acc = state[a]
        later = sums[2 * a * tq:(2 * a + 1) * tq] + sums[(2 * a + 1) * tq:(2 * a + 2) * tq]
        w = jnp.exp(zs[a] - sps[a] - later - carry)
        if mask is not None:
            w = jnp.where(mask, w, 0.0)
        out.append((carry + jnp.sum(sps[a], axis=1, keepdims=True), acc + _dot(w.astype(BF16), vs[a])))
    return tuple(out)


def _causal_mask(n):
    row = lax.broadcasted_iota(jnp.int32, (n, n), 0)
    col = lax.broadcasted_iota(jnp.int32, (n, n), 1)
    return col < row


def _sb_prompt_kernel(q_ref, k_ref, v_ref, tri_ref, o_ref, *, tq, dh, scale):
    qi = pl.program_id(2)
    heads = q_ref.shape[1] // dh
    tri = tri_ref[...]
    cols = [slice(a * dh, (a + 1) * dh) for a in range(heads)]
    qs = [q_ref[:, c] for c in cols]

    def visit(s0, state, mask):
        ks = [k_ref[0, pl.ds(s0, tq), c] for c in cols]
        vs = [v_ref[0, pl.ds(s0, tq), c] for c in cols]
        return _sb_blocks(qs, ks, vs, tri, state, mask, scale)

    state = tuple((jnp.zeros((tq, 1), F32), jnp.zeros((tq, dh), F32)) for _ in range(heads))
    state = visit(pl.multiple_of(qi * tq, tq), state, _causal_mask(tq))

    def live(st):
        lowest = functools.reduce(jnp.minimum, [jnp.min(c) for c, _ in st])
        return lowest < SB_CARRY_CUTOFF

    def cond(loop):
        i, more, _ = loop
        return jnp.logical_and(i < qi, more)

    def body(loop):
        i, _, st = loop
        st = visit(pl.multiple_of((qi - 1 - i) * tq, tq), st, None)
        return i + 1, live(st), st

    state = lax.while_loop(cond, body, (jnp.int32(0), live(state), state))[2]
    for a in range(heads):
        o_ref[:, cols[a]] = state[a][1].astype(o_ref.dtype)


def _sb_attention_prompt(q, kb, vb, tri, *, bsz, t, tq, heads_per_step):
    n, c = q.shape
    dh = c // SB_HEADS
    nq = t // tq
    w = heads_per_step * dh
    kv_spec = pl.BlockSpec((1, t, w), lambda b, g, i: (b, 0, g))
    return pl.pallas_call(
        functools.partial(_sb_prompt_kernel, tq=tq, dh=dh, scale=dh ** -0.5),
        out_shape=jax.ShapeDtypeStruct((n, c), BF16),
        grid=(bsz, SB_HEADS // heads_per_step, nq),
        in_specs=[
            pl.BlockSpec((tq, w), lambda b, g, i: (b * nq + i, g)),
            kv_spec, kv_spec,
            pl.BlockSpec((tq, tq), lambda b, g, i: (0, 0)),
        ],
        out_specs=pl.BlockSpec((tq, w), lambda b, g, i: (b * nq + i, g)),
        compiler_params=_params("parallel", "parallel", "arbitrary"),
        name="sb_attention_prompt",
    )(q, kb.reshape(bsz, t, c), vb.reshape(bsz, t, c), tri)


def _sb_sample_kernel(q_ref, kn_ref, vn_ref, kp_ref, vp_ref, trin_ref, trip_ref, o_ref, *, tk, dh, scale):
    tq = q_ref.shape[0]
    cols = [slice(a * dh, (a + 1) * dh) for a in range(SB_HEADS)]
    qs = [q_ref[:, c] for c in cols]
    tri_new = trin_ref[...]
    tri_past = trip_ref[...]
    state = tuple((jnp.zeros((tq, 1), F32), jnp.zeros((tq, dh), F32)) for _ in range(SB_HEADS))
    state = _sb_blocks(qs, [kn_ref[:, c] for c in cols], [vn_ref[:, c] for c in cols], tri_new,
                       state, _causal_mask(tq), scale)
    n_past = kp_ref.shape[1] // (tk * SB_HEADS)
    for i in range(n_past):
        s0 = (n_past - 1 - i) * tk * SB_HEADS
        rows = [pl.ds(s0 + a, tk, stride=SB_HEADS) for a in range(SB_HEADS)]
        state = _sb_blocks(qs, [kp_ref[0, r, :].astype(BF16) for r in rows],
                           [vp_ref[0, r, :].astype(BF16) for r in rows], tri_past, state, None, scale)
    for a in range(SB_HEADS):
        o_ref[:, cols[a]] = state[a][1].astype(o_ref.dtype)


def _sb_attention_sample(q, kb_new, vb_new, k_past, v_past, tri_new, tri_past, *, bsz, t, tk):
    n, c = q.shape
    dh = c // SB_HEADS
    rows_past = k_past.shape[1]
    new_spec = pl.BlockSpec((t, c), lambda b: (b, 0))
    past_spec = pl.BlockSpec((1, rows_past, dh), lambda b: (b, 0, 0))
    return pl.pallas_call(
        functools.partial(_sb_sample_kernel, tk=tk, dh=dh, scale=dh ** -0.5),
        out_shape=jax.ShapeDtypeStruct((n, c), BF16),
        grid=(bsz,),
        in_specs=[
            new_spec, new_spec, new_spec, past_spec, past_spec,
            pl.BlockSpec((t, t), lambda b: (0, 0)),
            pl.BlockSpec((tk, tk), lambda b: (0, 0)),
        ],
        out_specs=new_spec,
        compiler_params=_params("parallel"),
        name="sb_attention_sample",
    )(q, kb_new, vb_new, k_past, v_past, tri_new, tri_past)


def _strict_lower_ones(n):
    idx = jnp.arange(n)
    return (idx[:, None] > idx[None, :]).astype(BF16)


def _out_proj_kernel(x_ref, yc_ref, ys_ref, wc_ref, ws_ref, g_ref, b_ref, o_ref, *, alpha):
    for rows in _row_parts(x_ref.shape[0]):
        mix = _dot(yc_ref[rows, :], wc_ref[...]) + _dot(ys_ref[rows, :], ws_ref[...])
        o_ref[rows, :] = _layer_norm(alpha * x_ref[rows, :] + mix, g_ref[...], b_ref[...])


def _out_proj_post_norm(x, y_conv, y_sb, w_out, g, b, *, alpha, tm):
    n, d = x.shape
    c = y_conv.shape[1]
    vec = pl.BlockSpec((1, d), lambda i: (0, 0))
    return pl.pallas_call(
        functools.partial(_out_proj_kernel, alpha=alpha),
        out_shape=jax.ShapeDtypeStruct((n, d), F32),
        grid=(n // tm,),
        in_specs=[
            pl.BlockSpec((tm, d), lambda i: (i, 0)),
            pl.BlockSpec((tm, c), lambda i: (i, 0)),
            pl.BlockSpec((tm, c), lambda i: (i, 0)),
            pl.BlockSpec((c, d), lambda i: (0, 0)),
            pl.BlockSpec((c, d), lambda i: (1, 0)),
            vec, vec,
        ],
        out_specs=pl.BlockSpec((tm, d), lambda i: (i, 0)),
        compiler_params=_params("parallel"),
        name="out_proj_post_norm",
    )(x, y_conv, y_sb, w_out, w_out, g, b)


def _mem_kv_kernel(m_ref, wk_ref, wv_ref, k_ref, v_ref, ks_ref, vs_ref):
    mb = m_ref[...].astype(BF16)
    nreq, m, heads, dh = ks_ref.shape
    for w_ref, flat_ref, state_ref in ((wk_ref, k_ref, ks_ref), (wv_ref, v_ref, vs_ref)):
        r = _dot(mb, w_ref[...])
        flat_ref[...] = r
        for i in range(nreq):
            for h in range(heads):
                state_ref[i, :, h, :] = r[i * m:(i + 1) * m, h * dh:(h + 1) * dh]


def _memory_kv(mem, wk, wv, *, tm):
    bsz, m, d = mem.shape
    c = wk.shape[1]
    dh = c // XATTN_HEADS
    w_spec = pl.BlockSpec((d, c), lambda i: (0, 0))
    flat_spec = pl.BlockSpec((tm, c), lambda i: (i, 0))
    state_spec = pl.BlockSpec((tm // m, m, XATTN_HEADS, dh), lambda i: (i, 0, 0, 0))
    flat = jax.ShapeDtypeStruct((bsz * m, c), F32)
    state = jax.ShapeDtypeStruct((bsz, m, XATTN_HEADS, dh), F32)
    return pl.pallas_call(
        _mem_kv_kernel,
        out_shape=(flat, flat, state, state),
        grid=(bsz * m // tm,),
        in_specs=[pl.BlockSpec((tm, d), lambda i: (i, 0)), w_spec, w_spec],
        out_specs=(flat_spec, flat_spec, state_spec, state_spec),
        compiler_params=_params("parallel"),
        name="memory_kv",
    )(mem.reshape(bsz * m, d), wk, wv)


def _xattn_kernel(x_ref, mk_ref, mv_ref, wq_ref, wo_ref, g_ref, b_ref, o_ref, *, alpha, scale):
    mk = mk_ref[0].astype(BF16)
    mv = mv_ref[0].astype(BF16)
    dh = mk.shape[1] // XATTN_HEADS
    for rows in _row_parts(x_ref.shape[0]):
        x = x_ref[rows, :]
        q = _dot(x.astype(BF16), wq_ref[...])
        heads = []
        for h in range(XATTN_HEADS):
            sl = slice(h * dh, (h + 1) * dh)
            s = _dot_nt(q[:, sl].astype(BF16), mk[:, sl]) * scale
            e = jnp.exp(s - jnp.max(s, axis=-1, keepdims=True))
            p = e / jnp.sum(e, axis=-1, keepdims=True)
            heads.append(_dot(p.astype(BF16), mv[:, sl]).astype(BF16))
        o = jnp.concatenate(heads, axis=1)
        y = alpha * x + _dot(o, wo_ref[...])
        o_ref[rows, :] = _layer_norm(y, g_ref[...], b_ref[...])


def _xattn_post_norm(x, mem_k, mem_v, wq, wo, g, b, *, alpha, bsz, t, tm):
    n, d = x.shape
    m, c = mem_k.shape[1], mem_k.shape[2]
    nt = t // tm
    vec = pl.BlockSpec((1, d), lambda bi, i: (0, 0))
    mem_spec = pl.BlockSpec((1, m, c), lambda bi, i: (bi, 0, 0))
    return pl.pallas_call(
        functools.partial(_xattn_kernel, alpha=alpha, scale=(c // XATTN_HEADS) ** -0.5),
        out_shape=jax.ShapeDtypeStruct((n, d), F32),
        grid=(bsz, nt),
        in_specs=[
            pl.BlockSpec((tm, d), lambda bi, i: (bi * nt + i, 0)),
            mem_spec, mem_spec,
            pl.BlockSpec((d, c), lambda bi, i: (0, 0)),
            pl.BlockSpec((c, d), lambda bi, i: (0, 0)),
            vec, vec,
        ],
        out_specs=pl.BlockSpec((tm, d), lambda bi, i: (bi * nt + i, 0)),
        compiler_params=_params("parallel", "parallel"),
        name="xattn_post_norm",
    )(x, mem_k, mem_v, wq, wo, g, b)


LATER_MATRICES = ("w_in", "w_out", "xq", "xo", "f2g", "f2u", "f2d")


def _encoder_layer(x3, conv_prev, sb_past, mem_k, mem_v, w, *, alpha):
    bsz, t, d = x3.shape
    x = x3.reshape(bsz * t, d)
    tm = min(bsz * t, TOKEN_ROWS)
    tq = SB_BLOCK
    pending = [name for name in LATER_MATRICES if w[name].dtype != BF16]
    ffn_tm = min(bsz * t, FFN_ROWS)
    x, cast = _ffn_post_norm(x, w["f1g"], w["f1u"], w["f1d"], w["ln1g"], w["ln1b"], alpha=alpha, tm=ffn_tm,
                             tf=FFN_HIDDEN_TILE,
                             cast=tuple(w[name] for name in pending))
    w = {**w, **dict(zip(pending, cast))}
    seg = min(t, IN_PROJ_ROWS)
    y_conv, conv_new, q, k, v, kb, vb = _in_proj_conv(
        x, w["w_in"], conv_prev, w["conv_w"], w["conv_b"], w["cln_g"], w["cln_b"],
        tm=IN_PROJ_ROWS, seg=seg, steps_per_seq=t // seg)
    c = y_conv.shape[1]
    if sb_past is None:
        y_sb = _sb_attention_prompt(q, kb, vb, _strict_lower_ones(tq), bsz=bsz, t=t, tq=tq,
                                    heads_per_step=SB_HEADS_PER_STEP)
    else:
        k_past, v_past = sb_past
        y_sb = _sb_attention_sample(q, kb, vb, k_past, v_past, _strict_lower_ones(t), _strict_lower_ones(tq),
                                    bsz=bsz, t=t, tk=tq)
    x = _out_proj_post_norm(x, y_conv, y_sb, w["w_out"], w["ln2g"], w["ln2b"], alpha=alpha, tm=tm)
    x = _xattn_post_norm(x, mem_k, mem_v, w["xq"], w["xo"], w["ln3g"], w["ln3b"],
                         alpha=alpha, bsz=bsz, t=t, tm=min(tm, t))
    x, _ = _ffn_post_norm(x, w["f2g"], w["f2u"], w["f2d"], w["ln4g"], w["ln4b"], alpha=alpha, tm=ffn_tm,
                          tf=FFN_HIDDEN_TILE)
    dh = c // SB_HEADS
    return (x.reshape(bsz, t, d), conv_new, k.reshape(bsz, t, SB_HEADS, dh), v.reshape(bsz, t, SB_HEADS, dh)), w


def kernel(x_prompt, x_sample, mem_prompt, cache_conv, cache_sb_k, cache_sb_v, cache_mem_k, cache_mem_v,
           ffn1_w_gate, ffn1_w_up, ffn1_w_down, ln1_g, ln1_b, w_in, conv_w, conv_b, conv_ln_g, conv_ln_b,
           w_out, ln2_g, ln2_b, xattn_wq, xattn_wk, xattn_wv, xattn_wo, ln3_g, ln3_b,
           ffn2_w_gate, ffn2_w_up, ffn2_w_down, ln4_g, ln4_b):
    depth = ffn1_w_gate.shape[0]
    alpha = (2.0 * depth) ** 0.25
    bp, tp, d = x_prompt.shape
    bs, ts, _ = x_sample.shape
    conv_dim = conv_w.shape[2]
    xattn_dim = xattn_wq.shape[2]

    xp, xs = x_prompt, x_sample
    conv_p, k_p, v_p, mk_p, mv_p, conv_s, k_s, v_s = [], [], [], [], [], [], [], []
    for l in range(depth):
        row = lambda a: a[l][None, :]
        w = dict(
            f1g=ffn1_w_gate[l].astype(BF16), f1u=ffn1_w_up[l].astype(BF16), f1d=ffn1_w_down[l].astype(BF16),
            ln1g=row(ln1_g), ln1b=row(ln1_b), w_in=w_in[l],
            conv_w=conv_w[l], conv_b=row(conv_b), cln_g=row(conv_ln_g), cln_b=row(conv_ln_b),
            w_out=w_out[l], ln2g=row(ln2_g), ln2b=row(ln2_b),
            xq=xattn_wq[l], xo=xattn_wo[l], ln3g=row(ln3_g), ln3b=row(ln3_b),
            f2g=ffn2_w_gate[l], f2u=ffn2_w_up[l], f2d=ffn2_w_down[l],
            ln4g=row(ln4_g), ln4b=row(ln4_b),
        )
        m = mem_prompt.shape[1]
        mk, mv, mk_state, mv_state = _memory_kv(mem_prompt, xattn_wk[l].astype(BF16), xattn_wv[l].astype(BF16),
                                                tm=TOKEN_ROWS)
        mk = mk.reshape(bp, m, xattn_dim)
        mv = mv.reshape(bp, m, xattn_dim)
        (xp, cp, kp, vp), w = _encoder_layer(xp, jnp.zeros((bp, CONV_STATE, conv_dim), F32), None, mk, mv, w,
                                             alpha=alpha)
        conv_p.append(cp); k_p.append(kp); v_p.append(vp)
        mk_p.append(mk_state)
        mv_p.append(mv_state)

        p_len = cache_sb_k.shape[2]
        dh = cache_sb_k.shape[4]
        sb_past = (cache_sb_k[l].reshape(bs, p_len * SB_HEADS, dh), cache_sb_v[l].reshape(bs, p_len * SB_HEADS, dh))
        (xs, cs, ks, vs), _ = _encoder_layer(xs, cache_conv[l], sb_past,
                                             cache_mem_k[l].reshape(bs, m, xattn_dim),
                                             cache_mem_v[l].reshape(bs, m, xattn_dim), w,
                                             alpha=alpha)
        conv_s.append(cs); k_s.append(ks); v_s.append(vs)
    stack = lambda parts: parts[0][None] if len(parts) == 1 else jnp.stack(parts)
    return (xp, xs, stack(conv_p), stack(k_p), stack(v_p), stack(mk_p), stack(mv_p),
            stack(conv_s), stack(k_s), stack(v_s))
```

```python
import functools

import jax
import jax.numpy as jnp
from jax import lax
from jax.experimental import pallas as pl
from jax.experimental.pallas import tpu as pltpu

F32 = jnp.float32
BF16 = jnp.bfloat16

LN_EPS = 1e-5
LOG2_E = 1.4426950408889634
CONV_WIDTH = 31
CONV_STATE = CONV_WIDTH - 1
SB_HEADS = 8
XATTN_HEADS = 4
CONV_CTX_ROWS = 32
CONV_ROW_CHUNK = 32
CONV_SUBLANES = 8
MXU_ROWS = 256
CAST_SLAB_ROWS = 16
IN_PROJ_ROWS = 256
SB_CARRY_CUTOFF = 110.0
VMEM_LIMIT_BYTES = 56 * 1024 * 1024
FFN_VMEM_LIMIT_BYTES = 60 * 1024 * 1024
FFN_ROWS = 1024
FFN_HIDDEN_TILE = 512
TOKEN_ROWS = 512
SB_BLOCK = 256
SB_HEADS_PER_STEP = 8


def _layer_norm(y, g, b):
    mu = jnp.mean(y, axis=-1, keepdims=True)
    d = y - mu
    var = jnp.mean(d * d, axis=-1, keepdims=True)
    return d * lax.rsqrt(var + LN_EPS) * g + b


def _dot(a, b):
    return jnp.dot(a, b, preferred_element_type=F32)


def _dot_nt(a, b):
    return lax.dot_general(a, b, (((1,), (1,)), ((), ())), preferred_element_type=F32)


def _row_parts(tm):
    parts = 2 if tm % (2 * MXU_ROWS) == 0 else 1
    return [slice(r * tm // parts, (r + 1) * tm // parts) for r in range(parts)]


def _params(*sem, flags=None):
    return pltpu.CompilerParams(dimension_semantics=sem, vmem_limit_bytes=VMEM_LIMIT_BYTES, flags=flags)


def _ffn_kernel(x_ref, wg_ref, wu_ref, wd_ref, g_ref, b_ref, *rest, alpha, nf, n_cast):
    cast_in, o_ref, cast_out, acc_ref = rest[:n_cast], rest[n_cast], rest[n_cast + 1:-1], rest[-1]
    f = pl.program_id(1)

    def step(first, last):
        for src, dst in zip(cast_in, cast_out):
            dst[...] = src[...].astype(BF16)
        tm = x_ref.shape[0]
        parts = max(1, tm // MXU_ROWS) if last else 1
        for r in range(parts):
            rows = slice(r * tm // parts, (r + 1) * tm // parts)
            x = x_ref[rows, :]
            xb = x.astype(BF16)
            gate = _dot(xb, wg_ref[...])
            up = _dot(xb, wu_ref[...])
            h = (gate * jax.nn.sigmoid(gate) * up).astype(BF16)
            total = _dot(h, wd_ref[...])
            if not first:
                total = acc_ref[rows, :] + total
            if last:
                o_ref[rows, :] = _layer_norm(alpha * x + 0.5 * total, g_ref[...], b_ref[...])
            else:
                acc_ref[rows, :] = total

    if nf == 1:
        step(True, True)
    else:
        pl.when(f == 0)(lambda: step(True, False))
        pl.when(jnp.logical_and(f > 0, f < nf - 1))(lambda: step(False, False))
        pl.when(f == nf - 1)(lambda: step(False, True))


def _cast_slab_spec(rows, cols, n_tiles, nf):
    slab = CAST_SLAB_ROWS
    while rows // slab // n_tiles > nf:
        slab += CAST_SLAB_ROWS
    per_tile = rows // slab // n_tiles
    assert per_tile * slab * n_tiles == rows and per_tile >= 1, (rows, n_tiles, nf)
    return pl.BlockSpec((slab, cols), lambda i, f: (i * per_tile + jnp.minimum(f, per_tile - 1), 0))


def _ffn_post_norm(x, wg, wu, wd, g, b, *, alpha, tm, tf, cast=()):
    n, d = x.shape
    dff = wg.shape[1]
    nf = dff // tf
    cast_specs = [_cast_slab_spec(w.shape[0], w.shape[1], n // tm, nf) for w in cast]
    out = pl.pallas_call(
        functools.partial(_ffn_kernel, alpha=alpha, nf=nf, n_cast=len(cast)),
        out_shape=(jax.ShapeDtypeStruct((n, d), F32), *[jax.ShapeDtypeStruct(w.shape, BF16) for w in cast]),
        grid=(n // tm, nf),
        in_specs=[
            pl.BlockSpec((tm, d), lambda i, f: (i, 0)),
            pl.BlockSpec((d, tf), lambda i, f: (0, f)),
            pl.BlockSpec((d, tf), lambda i, f: (0, f)),
            pl.BlockSpec((tf, d), lambda i, f: (f, 0)),
            pl.BlockSpec((1, d), lambda i, f: (0, 0)),
            pl.BlockSpec((1, d), lambda i, f: (0, 0)),
            *cast_specs,
        ],
        out_specs=(pl.BlockSpec((tm, d), lambda i, f: (i, 0), pipeline_mode=pl.Buffered(1)), *cast_specs),
        scratch_shapes=[pltpu.VMEM((tm, d), F32)],
        compiler_params=pltpu.CompilerParams(dimension_semantics=("arbitrary", "arbitrary"),
                                             vmem_limit_bytes=FFN_VMEM_LIMIT_BYTES),
        name="ffn_post_norm",
    )(x, wg, wu, wd, g, b, *cast)
    return out[0], out[1:]


def _store_heads_interleaved(dst_ref, r):
    n, c = r.shape
    dh = c // SB_HEADS
    for h in range(SB_HEADS):
        dst_ref[pl.ds(h, n, stride=SB_HEADS), :] = r[:, h * dh:(h + 1) * dh]


def _in_proj_conv_kernel(x_ref, w_ref, prev_ref, cw_ref, cb_ref, g_ref, b_ref,
                         y_ref, st_ref, q_ref, k_ref, v_ref, kb_ref, vb_ref,
                         tail_ref, hp_ref, sh_ref, *, seg, steps_per_seq):
    ctx, sub, rc = CONV_CTX_ROWS, CONV_SUBLANES, CONV_ROW_CHUNK
    pad = ctx - CONV_STATE
    tm = x_ref.shape[0]
    c = y_ref.shape[1]
    nseg = tm // seg
    stride = ctx + seg

    def load_cached_context():
        for s in range(nseg):
            tail_ref[s * ctx:s * ctx + pad, :] = jnp.zeros((pad, c), F32)
            tail_ref[s * ctx + pad:(s + 1) * ctx, :] = prev_ref[s]

    if steps_per_seq == 1:
        load_cached_context()
    else:
        pl.when(pl.program_id(0) % steps_per_seq == 0)(load_cached_context)

    xb = x_ref[...].astype(BF16)
    group = lambda j: _dot(xb, w_ref[:, j * c:(j + 1) * c])
    h = group(0) * jax.nn.sigmoid(group(1))
    for s in range(nseg):
        hp_ref[s * stride:s * stride + ctx, :] = tail_ref[s * ctx:(s + 1) * ctx, :]
        hp_ref[s * stride + ctx:(s + 1) * stride, :] = h[s * seg:(s + 1) * seg, :]

    dh = k_ref.shape[1]

    def qkv_tile(j, l0):
        lanes = slice(l0, l0 + MXU_ROWS)
        r = _dot(xb, w_ref[:, j * c + l0:j * c + l0 + MXU_ROWS])
        if j == 2:
            q_ref[:, lanes] = r.astype(BF16)
            return
        state_ref, bf_ref = (k_ref, kb_ref) if j == 3 else (v_ref, vb_ref)
        for i in range(MXU_ROWS // dh):
            state_ref[pl.ds(l0 // dh + i, tm, stride=SB_HEADS), :] = r[:, i * dh:(i + 1) * dh]
        bf_ref[:, lanes] = r.astype(BF16)

    qkv_tiles = [(j, l0) for j in (2, 3, 4) for l0 in range(0, c, MXU_ROWS)]

    span = nseg * stride - sub
    for r in range(1, sub):
        sh_ref[r - 1, 0:span, :] = hp_ref[r:r + span, :]

    for tile in qkv_tiles:
        qkv_tile(*tile)

    groups = rc // sub
    for s in range(nseg):
        for ci in range(seg // rc):
            r0 = s * stride + ci * rc
            acc = [jnp.broadcast_to(cb_ref[...], (sub, c)) for _ in range(groups)]
            for j in range(CONV_WIDTH):
                a, r = divmod(pad + j, sub)
                wj = cw_ref[j]
                for gi in range(groups):
                    lo = r0 + (a + gi) * sub
                    rows = hp_ref[lo:lo + sub, :] if r == 0 else sh_ref[r - 1, lo:lo + sub, :]
                    acc[gi] = acc[gi] + wj * rows
            y = _layer_norm(jnp.concatenate(acc, axis=0), g_ref[...], b_ref[...])
            y_ref[s * seg + ci * rc:s * seg + (ci + 1) * rc, :] = (y * jax.nn.sigmoid(y)).astype(y_ref.dtype)

    for s in range(nseg):
        last = hp_ref[s * stride + seg:(s + 1) * stride, :]
        tail_ref[s * ctx:(s + 1) * ctx, :] = last
        st_ref[s] = last[pad:, :]


def _in_proj_conv(x, w_in, prev, conv_w, conv_b, ln_g, ln_b, *, tm, seg, steps_per_seq):
    n, d = x.shape
    cols = w_in.shape[1]
    c = cols // 5
    dh = c // SB_HEADS
    nseg = tm // seg
    nreq = prev.shape[0]
    hp_rows = nseg * (CONV_CTX_ROWS + seg)
    out_block = pl.BlockSpec((tm, c), lambda i: (i, 0))
    state_block = pl.BlockSpec((tm * SB_HEADS, dh), lambda i: (i, 0))
    conv_state_block = pl.BlockSpec((nseg, CONV_STATE, c), lambda i: (i // steps_per_seq, 0, 0))
    vec = pl.BlockSpec((1, c), lambda i: (0, 0))
    return pl.pallas_call(
        functools.partial(_in_proj_conv_kernel, seg=seg, steps_per_seq=steps_per_seq),
        out_shape=(
            jax.ShapeDtypeStruct((n, c), BF16),
            jax.ShapeDtypeStruct((nreq, CONV_STATE, c), F32),
            jax.ShapeDtypeStruct((n, c), BF16),
            jax.ShapeDtypeStruct((n * SB_HEADS, dh), F32),
            jax.ShapeDtypeStruct((n * SB_HEADS, dh), F32),
            jax.ShapeDtypeStruct((n, c), BF16),
            jax.ShapeDtypeStruct((n, c), BF16),
        ),
        grid=(n // tm,),
        in_specs=[
            pl.BlockSpec((tm, d), lambda i: (i, 0)),
            pl.BlockSpec((d, cols), lambda i: (0, 0), pipeline_mode=pl.Buffered(1)),
            conv_state_block,
            pl.BlockSpec((CONV_WIDTH, CONV_SUBLANES, c), lambda i: (0, 0, 0)),
            vec, vec, vec,
        ],
        out_specs=(out_block, conv_state_block, out_block, state_block, state_block, out_block, out_block),
        scratch_shapes=[pltpu.VMEM((nseg * CONV_CTX_ROWS, c), F32),
                        pltpu.VMEM((hp_rows, c), F32),
                        pltpu.VMEM((CONV_SUBLANES - 1, hp_rows - CONV_SUBLANES, c), F32)],
        compiler_params=_params("arbitrary"),
        name="in_proj_conv",
    )(x, w_in, prev, jnp.broadcast_to(conv_w[:, None, :], (CONV_WIDTH, CONV_SUBLANES, c)), conv_b, ln_g, ln_b)


def _sb_blocks(qs, ks, vs, tri, state, mask, scale):
    n = len(qs)
    tq = qs[0].shape[0]
    zs = [_dot_nt(qs[a], ks[a]) * scale for a in range(n)]
    sps = []
    for z in zs:
        sp = jnp.maximum(z, 0.0) + jnp.log(1.0 + jnp.exp2(jnp.abs(z) * -LOG2_E))
        sps.append(sp if mask is None else jnp.where(mask, sp, 0.0))
    parts = []
    for sp in sps:
        hi = sp.astype(BF16)
        parts += [hi, (sp - hi.astype(F32)).astype(BF16)]
    sums = _dot(jnp.concatenate(parts, axis=0), tri)
    out = []
    for a in range(n):
        carry, acc = state[a]
        later = sums[2 * a * tq:(2 * a + 1) * tq] + sums[(2 * a + 1) * tq:(2 * a + 2) * tq]
        w = jnp.exp(zs[a] - sps[a] - later - carry)
        if mask is not None:
            w = jnp.where(mask, w, 0.0)
        out.append((carry + jnp.sum(sps[a], axis=1, keepdims=True), acc + _dot(w.astype(BF16), vs[a])))
    return tuple(out)


def _causal_mask(n):
    row = lax.broadcasted_iota(jnp.int32, (n, n), 0)
    col = lax.broadcasted_iota(jnp.int32, (n, n), 1)
    return col < row


def _sb_prompt_kernel(q_ref, k_ref, v_ref, tri_ref, o_ref, *, tq, dh, scale):
    qi = pl.program_id(2)
    heads = q_ref.shape[1] // dh
    tri = tri_ref[...]
    cols = [slice(a * dh, (a + 1) * dh) for a in range(heads)]
    qs = [q_ref[:, c] for c in cols]

    def visit(s0, state, mask):
        ks = [k_ref[0, pl.ds(s0, tq), c] for c in cols]
        vs = [v_ref[0, pl.ds(s0, tq), c] for c in cols]
        return _sb_blocks(qs, ks, vs, tri, state, mask, scale)

    state = tuple((jnp.zeros((tq, 1), F32), jnp.zeros((tq, dh), F32)) for _ in range(heads))
    state = visit(pl.multiple_of(qi * tq, tq), state, _causal_mask(tq))

    def live(st):
        lowest = functools.reduce(jnp.minimum, [jnp.min(c) for c, _ in st])
        return lowest < SB_CARRY_CUTOFF

    def cond(loop):
        i, more, _ = loop
        return jnp.logical_and(i < qi, more)

    def body(loop):
        i, _, st = loop
        st = visit(pl.multiple_of((qi - 1 - i) * tq, tq), st, None)
        return i + 1, live(st), st

    state = lax.while_loop(cond, body, (jnp.int32(0), live(state), state))[2]
    for a in range(heads):
        o_ref[:, cols[a]] = state[a][1].astype(o_ref.dtype)


def _sb_attention_prompt(q, kb, vb, tri, *, bsz, t, tq, heads_per_step):
    n, c = q.shape
    dh = c // SB_HEADS
    nq = t // tq
    w = heads_per_step * dh
    kv_spec = pl.BlockSpec((1, t, w), lambda b, g, i: (b, 0, g))
    return pl.pallas_call(
        functools.partial(_sb_prompt_kernel, tq=tq, dh=dh, scale=dh ** -0.5),
        out_shape=jax.ShapeDtypeStruct((n, c), BF16),
        grid=(bsz, SB_HEADS // heads_per_step, nq),
        in_specs=[
            pl.BlockSpec((tq, w), lambda b, g, i: (b * nq + i, g)),
            kv_spec, kv_spec,
            pl.BlockSpec((tq, tq), lambda b, g, i: (0, 0)),
        ],
        out_specs=pl.BlockSpec((tq, w), lambda b, g, i: (b * nq + i, g)),
        compiler_params=_params("parallel", "parallel", "arbitrary"),
        name="sb_attention_prompt",
    )(q, kb.reshape(bsz, t, c), vb.reshape(bsz, t, c), tri)


def _sb_sample_kernel(q_ref, kn_ref, vn_ref, kp_ref, vp_ref, trin_ref, trip_ref, o_ref, *, tk, dh, scale):
    tq = q_ref.shape[0]
    cols = [slice(a * dh, (a + 1) * dh) for a in range(SB_HEADS)]
    qs = [q_ref[:, c] for c in cols]
    tri_new = trin_ref[...]
    tri_past = trip_ref[...]
    state = tuple((jnp.zeros((tq, 1), F32), jnp.zeros((tq, dh), F32)) for _ in range(SB_HEADS))
    state = _sb_blocks(qs, [kn_ref[:, c] for c in cols], [vn_ref[:, c] for c in cols], tri_new,
                       state, _causal_mask(tq), scale)
    n_past = kp_ref.shape[1] // (tk * SB_HEADS)
    for i in range(n_past):
        s0 = (n_past - 1 - i) * tk * SB_HEADS
        rows = [pl.ds(s0 + a, tk, stride=SB_HEADS) for a in range(SB_HEADS)]
        state = _sb_blocks(qs, [kp_ref[0, r, :].astype(BF16) for r in rows],
                           [vp_ref[0, r, :].astype(BF16) for r in rows], tri_past, state, None, scale)
    for a in range(SB_HEADS):
        o_ref[:, cols[a]] = state[a][1].astype(o_ref.dtype)


def _sb_attention_sample(q, kb_new, vb_new, k_past, v_past, tri_new, tri_past, *, bsz, t, tk):
    n, c = q.shape
    dh = c // SB_HEADS
    rows_past = k_past.shape[1]
    new_spec = pl.BlockSpec((t, c), lambda b: (b, 0))
    past_spec = pl.BlockSpec((1, rows_past, dh), lambda b: (b, 0, 0))
    return pl.pallas_call(
        functools.partial(_sb_sample_kernel, tk=tk, dh=dh, scale=dh ** -0.5),
        out_shape=jax.ShapeDtypeStruct((n, c), BF16),
        grid=(bsz,),
        in_specs=[
            new_spec, new_spec, new_spec, past_spec, past_spec,
            pl.BlockSpec((t, t), lambda b: (0, 0)),
            pl.BlockSpec((tk, tk), lambda b: (0, 0)),
        ],
        out_specs=new_spec,
        compiler_params=_params("parallel"),
        name="sb_attention_sample",
    )(q, kb_new, vb_new, k_past, v_past, tri_new, tri_past)


def _strict_lower_ones(n):
    idx = jnp.arange(n)
    return (idx[:, None] > idx[None, :]).astype(BF16)


def _out_proj_kernel(x_ref, yc_ref, ys_ref, wc_ref, ws_ref, g_ref, b_ref, o_ref, *, alpha):
    for rows in _row_parts(x_ref.shape[0]):
        mix = _dot(yc_ref[rows, :], wc_ref[...]) + _dot(ys_ref[rows, :], ws_ref[...])
        o_ref[rows, :] = _layer_norm(alpha * x_ref[rows, :] + mix, g_ref[...], b_ref[...])


def _out_proj_post_norm(x, y_conv, y_sb, w_out, g, b, *, alpha, tm):
    n, d = x.shape
    c = y_conv.shape[1]
    vec = pl.BlockSpec((1, d), lambda i: (0, 0))
    return pl.pallas_call(
        functools.partial(_out_proj_kernel, alpha=alpha),
        out_shape=jax.ShapeDtypeStruct((n, d), F32),
        grid=(n // tm,),
        in_specs=[
            pl.BlockSpec((tm, d), lambda i: (i, 0)),
            pl.BlockSpec((tm, c), lambda i: (i, 0)),
            pl.BlockSpec((tm, c), lambda i: (i, 0)),
            pl.BlockSpec((c, d), lambda i: (0, 0)),
            pl.BlockSpec((c, d), lambda i: (1, 0)),
            vec, vec,
        ],
        out_specs=pl.BlockSpec((tm, d), lambda i: (i, 0)),
        compiler_params=_params("parallel"),
        name="out_proj_post_norm",
    )(x, y_conv, y_sb, w_out, w_out, g, b)


def _mem_kv_kernel(m_ref, wk_ref, wv_ref, k_ref, v_ref, ks_ref, vs_ref):
    mb = m_ref[...].astype(BF16)
    nreq, m, heads, dh = ks_ref.shape
    for w_ref, flat_ref, state_ref in ((wk_ref, k_ref, ks_ref), (wv_ref, v_ref, vs_ref)):
        r = _dot(mb, w_ref[...])
        flat_ref[...] = r
        for i in range(nreq):
            for h in range(heads):
                state_ref[i, :, h, :] = r[i * m:(i + 1) * m, h * dh:(h + 1) * dh]


def _memory_kv(mem, wk, wv, *, tm):
    bsz, m, d = mem.shape
    c = wk.shape[1]
    dh = c // XATTN_HEADS
    w_spec = pl.BlockSpec((d, c), lambda i: (0, 0))
    flat_spec = pl.BlockSpec((tm, c), lambda i: (i, 0))
    state_spec = pl.BlockSpec((tm // m, m, XATTN_HEADS, dh), lambda i: (i, 0, 0, 0))
    flat = jax.ShapeDtypeStruct((bsz * m, c), F32)
    state = jax.ShapeDtypeStruct((bsz, m, XATTN_HEADS, dh), F32)
    return pl.pallas_call(
        _mem_kv_kernel,
        out_shape=(flat, flat, state, state),
        grid=(bsz * m // tm,),
        in_specs=[pl.BlockSpec((tm, d), lambda i: (i, 0)), w_spec, w_spec],
        out_specs=(flat_spec, flat_spec, state_spec, state_spec),
        compiler_params=_params("parallel"),
        name="memory_kv",
    )(mem.reshape(bsz * m, d), wk, wv)


def _xattn_kernel(x_ref, mk_ref, mv_ref, wq_ref, wo_ref, g_ref, b_ref, o_ref, *, alpha, scale):
    nreq = mk_ref.shape[0]
    dh = mk_ref.shape[2] // XATTN_HEADS
    mem = [(mk_ref[i].astype(BF16), mv_ref[i].astype(BF16)) for i in range(nreq)]
    for rows in (_row_parts(x_ref.shape[0]) if nreq == 1 else [slice(0, x_ref.shape[0])]):
        x = x_ref[rows, :]
        q = _dot(x.astype(BF16), wq_ref[...])
        seg = q.shape[0] // nreq
        per_request = []
        for i, (mk, mv) in enumerate(mem):
            heads = []
            for h in range(XATTN_HEADS):
                sl = slice(h * dh, (h + 1) * dh)
                s = _dot_nt(q[i * seg:(i + 1) * seg, sl].astype(BF16), mk[:, sl]) * scale
                e = jnp.exp(s - jnp.max(s, axis=-1, keepdims=True))
                p = e / jnp.sum(e, axis=-1, keepdims=True)
                heads.append(_dot(p.astype(BF16), mv[:, sl]).astype(BF16))
            per_request.append(jnp.concatenate(heads, axis=1))
        o = jnp.concatenate(per_request, axis=0)
        y = alpha * x + _dot(o, wo_ref[...])
        o_ref[rows, :] = _layer_norm(y, g_ref[...], b_ref[...])


def _xattn_post_norm(x, mem_k, mem_v, wq, wo, g, b, *, alpha, t, tm):
    n, d = x.shape
    m, c = mem_k.shape[1], mem_k.shape[2]
    nreq = max(1, tm // t)
    tiles_per_request = max(1, t // tm)
    vec = pl.BlockSpec((1, d), lambda i: (0, 0))
    mem_spec = pl.BlockSpec((nreq, m, c), lambda i: (i // tiles_per_request, 0, 0))
    return pl.pallas_call(
        functools.partial(_xattn_kernel, alpha=alpha, scale=(c // XATTN_HEADS) ** -0.5),
        out_shape=jax.ShapeDtypeStruct((n, d), F32),
        grid=(n // tm,),
        in_specs=[
            pl.BlockSpec((tm, d), lambda i: (i, 0)),
            mem_spec, mem_spec,
            pl.BlockSpec((d, c), lambda i: (0, 0)),
            pl.BlockSpec((c, d), lambda i: (0, 0)),
            vec, vec,
        ],
        out_specs=pl.BlockSpec((tm, d), lambda i: (i, 0)),
        compiler_params=_params("parallel"),
        name="xattn_post_norm",
    )(x, mem_k, mem_v, wq, wo, g, b)


LATER_MATRICES = ("w_in", "w_out", "xq", "xo", "f2g", "f2u", "f2d")


def _encoder_layer(x3, conv_prev, sb_past, mem_k, mem_v, w, *, alpha):
    bsz, t, d = x3.shape
    x = x3.reshape(bsz * t, d)
    tm = min(bsz * t, TOKEN_ROWS)
    tq = SB_BLOCK
    pending = [name for name in LATER_MATRICES if w[name].dtype != BF16]
    ffn_tm = min(bsz * t, FFN_ROWS)
    x, cast = _ffn_post_norm(x, w["f1g"], w["f1u"], w["f1d"], w["ln1g"], w["ln1b"], alpha=alpha, tm=ffn_tm,
                             tf=FFN_HIDDEN_TILE,
                             cast=tuple(w[name] for name in pending))
    w = {**w, **dict(zip(pending, cast))}
    seg = min(t, IN_PROJ_ROWS)
    y_conv, conv_new, q, k, v, kb, vb = _in_proj_conv(
        x, w["w_in"], conv_prev, w["conv_w"], w["conv_b"], w["cln_g"], w["cln_b"],
        tm=IN_PROJ_ROWS, seg=seg, steps_per_seq=t // seg)
    c = y_conv.shape[1]
    if sb_past is None:
        y_sb = _sb_attention_prompt(q, kb, vb, _strict_lower_ones(tq), bsz=bsz, t=t, tq=tq,
                                    heads_per_step=SB_HEADS_PER_STEP)
    else:
        k_past, v_past = sb_past
        y_sb = _sb_attention_sample(q, kb, vb, k_past, v_past, _strict_lower_ones(t), _strict_lower_ones(tq),
                                    bsz=bsz, t=t, tk=tq)
    x = _out_proj_post_norm(x, y_conv, y_sb, w["w_out"], w["ln2g"], w["ln2b"], alpha=alpha, tm=tm)
    x = _xattn_post_norm(x, mem_k, mem_v, w["xq"], w["xo"], w["ln3g"], w["ln3b"], alpha=alpha, t=t, tm=tm)
    x, _ = _ffn_post_norm(x, w["f2g"], w["f2u"], w["f2d"], w["ln4g"], w["ln4b"], alpha=alpha, tm=ffn_tm,
                          tf=FFN_HIDDEN_TILE)
    dh = c // SB_HEADS
    return (x.reshape(bsz, t, d), conv_new, k.reshape(bsz, t, SB_HEADS, dh), v.reshape(bsz, t, SB_HEADS, dh)), w


def kernel(x_prompt, x_sample, mem_prompt, cache_conv, cache_sb_k, cache_sb_v, cache_mem_k, cache_mem_v,
           ffn1_w_gate, ffn1_w_up, ffn1_w_down, ln1_g, ln1_b, w_in, conv_w, conv_b, conv_ln_g, conv_ln_b,
           w_out, ln2_g, ln2_b, xattn_wq, xattn_wk, xattn_wv, xattn_wo, ln3_g, ln3_b,
           ffn2_w_gate, ffn2_w_up, ffn2_w_down, ln4_g, ln4_b):
    depth = ffn1_w_gate.shape[0]
    alpha = (2.0 * depth) ** 0.25
    bp, tp, d = x_prompt.shape
    bs, ts, _ = x_sample.shape
    conv_dim = conv_w.shape[2]
    xattn_dim = xattn_wq.shape[2]

    xp, xs = x_prompt, x_sample
    conv_p, k_p, v_p, mk_p, mv_p, conv_s, k_s, v_s = [], [], [], [], [], [], [], []
    for l in range(depth):
        row = lambda a: a[l][None, :]
        w = dict(
            f1g=ffn1_w_gate[l].astype(BF16), f1u=ffn1_w_up[l].astype(BF16), f1d=ffn1_w_down[l].astype(BF16),
            ln1g=row(ln1_g), ln1b=row(ln1_b), w_in=w_in[l],
            conv_w=conv_w[l], conv_b=row(conv_b), cln_g=row(conv_ln_g), cln_b=row(conv_ln_b),
            w_out=w_out[l], ln2g=row(ln2_g), ln2b=row(ln2_b),
            xq=xattn_wq[l], xo=xattn_wo[l], ln3g=row(ln3_g), ln3b=row(ln3_b),
            f2g=ffn2_w_gate[l], f2u=ffn2_w_up[l], f2d=ffn2_w_down[l],
            ln4g=row(ln4_g), ln4b=row(ln4_b),
        )
        m = mem_prompt.shape[1]
        mk, mv, mk_state, mv_state = _memory_kv(mem_prompt, xattn_wk[l].astype(BF16), xattn_wv[l].astype(BF16),
                                                tm=TOKEN_ROWS)
        mk = mk.reshape(bp, m, xattn_dim)
        mv = mv.reshape(bp, m, xattn_dim)
        (xp, cp, kp, vp), w = _encoder_layer(xp, jnp.zeros((bp, CONV_STATE, conv_dim), F32), None, mk, mv, w,
                                             alpha=alpha)
        conv_p.append(cp); k_p.append(kp); v_p.append(vp)
        mk_p.append(mk_state)
        mv_p.append(mv_state)

        p_len = cache_sb_k.shape[2]
        dh = cache_sb_k.shape[4]
        sb_past = (cache_sb_k[l].reshape(bs, p_len * SB_HEADS, dh), cache_sb_v[l].reshape(bs, p_len * SB_HEADS, dh))
        (xs, cs, ks, vs), _ = _encoder_layer(xs, cache_conv[l], sb_past,
                                             cache_mem_k[l].reshape(bs, m, xattn_dim),
                                             cache_mem_v[l].reshape(bs, m, xattn_dim), w,
                                             alpha=alpha)
        conv_s.append(cs); k_s.append(ks); v_s.append(vs)
    stack = lambda parts: parts[0][None] if len(parts) == 1 else jnp.stack(parts)
    return (xp, xs, stack(conv_p), stack(k_p), stack(v_p), stack(mk_p), stack(mv_p),
            stack(conv_s), stack(k_s), stack(v_s))
```

```python
import functools

import jax
import jax.numpy as jnp
from jax import lax
from jax.experimental import pallas as pl
from jax.experimental.pallas import tpu as pltpu

F32 = jnp.float32
BF16 = jnp.bfloat16

LN_EPS = 1e-5
LOG2_E = 1.4426950408889634
CONV_WIDTH = 31
CONV_STATE = CONV_WIDTH - 1
SB_HEADS = 8
XATTN_HEADS = 4
CONV_CTX_ROWS = 32
CONV_ROW_CHUNK = 32
CONV_SUBLANES = 8
MXU_ROWS = 256
CAST_SLAB_ROWS = 16
IN_PROJ_ROWS = 256
SB_CARRY_CUTOFF = 110.0
VMEM_LIMIT_BYTES = 56 * 1024 * 1024
FFN_VMEM_LIMIT_BYTES = 60 * 1024 * 1024
FFN_ROWS = 1024
FFN_HIDDEN_TILE = 512
OUT_PROJ_ROWS = 512
XATTN_ROWS = 1024
MEM_KV_ROWS = 512
SB_BLOCK = 256
SB_HEADS_PER_STEP = 8


def _layer_norm(y, g, b):
    mu = jnp.mean(y, axis=-1, keepdims=True)
    d = y - mu
    var = jnp.mean(d * d, axis=-1, keepdims=True)
    return d * lax.rsqrt(var + LN_EPS) * g + b


def _dot(a, b):
    return jnp.dot(a, b, preferred_element_type=F32)


def _dot_nt(a, b):
    return lax.dot_general(a, b, (((1,), (1,)), ((), ())), preferred_element_type=F32)


def _row_parts(tm):
    parts = 2 if tm % (2 * MXU_ROWS) == 0 else 1
    return [slice(r * tm // parts, (r + 1) * tm // parts) for r in range(parts)]


def _params(*sem, flags=None):
    return pltpu.CompilerParams(dimension_semantics=sem, vmem_limit_bytes=VMEM_LIMIT_BYTES, flags=flags)


def _ffn_kernel(x_ref, wg_ref, wu_ref, wd_ref, g_ref, b_ref, *rest, alpha, nf, n_cast):
    cast_in, o_ref, cast_out, acc_ref = rest[:n_cast], rest[n_cast], rest[n_cast + 1:-1], rest[-1]
    f = pl.program_id(1)

    def step(first, last):
        for src, dst in zip(cast_in, cast_out):
            dst[...] = src[...].astype(BF16)
        tm = x_ref.shape[0]
        parts = max(1, tm // MXU_ROWS) if last else 1
        for r in range(parts):
            rows = slice(r * tm // parts, (r + 1) * tm // parts)
            x = x_ref[rows, :]
            xb = x.astype(BF16)
            gate = _dot(xb, wg_ref[...])
            up = _dot(xb, wu_ref[...])
            h = (gate * jax.nn.sigmoid(gate) * up).astype(BF16)
            total = _dot(h, wd_ref[...])
            if not first:
                total = acc_ref[rows, :] + total
            if last:
                o_ref[rows, :] = _layer_norm(alpha * x + 0.5 * total, g_ref[...], b_ref[...])
            else:
                acc_ref[rows, :] = total

    if nf == 1:
        step(True, True)
    else:
        pl.when(f == 0)(lambda: step(True, False))
        pl.when(jnp.logical_and(f > 0, f < nf - 1))(lambda: step(False, False))
        pl.when(f == nf - 1)(lambda: step(False, True))


def _cast_slab_spec(rows, cols, n_tiles, nf):
    slab = CAST_SLAB_ROWS
    while rows // slab // n_tiles > nf:
        slab += CAST_SLAB_ROWS
    per_tile = rows // slab // n_tiles
    assert per_tile * slab * n_tiles == rows and per_tile >= 1, (rows, n_tiles, nf)
    return pl.BlockSpec((slab, cols), lambda i, f: (i * per_tile + jnp.minimum(f, per_tile - 1), 0))


def _ffn_post_norm(x, wg, wu, wd, g, b, *, alpha, tm, tf, cast=()):
    n, d = x.shape
    dff = wg.shape[1]
    nf = dff // tf
    cast_specs = [_cast_slab_spec(w.shape[0], w.shape[1], n // tm, nf) for w in cast]
    out = pl.pallas_call(
        functools.partial(_ffn_kernel, alpha=alpha, nf=nf, n_cast=len(cast)),
        out_shape=(jax.ShapeDtypeStruct((n, d), F32), *[jax.ShapeDtypeStruct(w.shape, BF16) for w in cast]),
        grid=(n // tm, nf),
        in_specs=[
            pl.BlockSpec((tm, d), lambda i, f: (i, 0)),
            pl.BlockSpec((d, tf), lambda i, f: (0, f)),
            pl.BlockSpec((d, tf), lambda i, f: (0, f)),
            pl.BlockSpec((tf, d), lambda i, f: (f, 0)),
            pl.BlockSpec((1, d), lambda i, f: (0, 0)),
            pl.BlockSpec((1, d), lambda i, f: (0, 0)),
            *cast_specs,
        ],
        out_specs=(pl.BlockSpec((tm, d), lambda i, f: (i, 0), pipeline_mode=pl.Buffered(1)), *cast_specs),
        scratch_shapes=[pltpu.VMEM((tm, d), F32)],
        compiler_params=pltpu.CompilerParams(dimension_semantics=("arbitrary", "arbitrary"),
                                             vmem_limit_bytes=FFN_VMEM_LIMIT_BYTES),
        name="ffn_post_norm",
    )(x, wg, wu, wd, g, b, *cast)
    return out[0], out[1:]


def _store_heads_interleaved(dst_ref, r):
    n, c = r.shape
    dh = c // SB_HEADS
    for h in range(SB_HEADS):
        dst_ref[pl.ds(h, n, stride=SB_HEADS), :] = r[:, h * dh:(h + 1) * dh]


def _in_proj_conv_kernel(x_ref, w_ref, prev_ref, cw_ref, cb_ref, g_ref, b_ref,
                         y_ref, st_ref, q_ref, k_ref, v_ref, kb_ref, vb_ref,
                         tail_ref, hp_ref, sh_ref, *, seg, steps_per_seq):
    ctx, sub, rc = CONV_CTX_ROWS, CONV_SUBLANES, CONV_ROW_CHUNK
    pad = ctx - CONV_STATE
    tm = x_ref.shape[0]
    c = y_ref.shape[1]
    nseg = tm // seg
    stride = ctx + seg

    def load_cached_context():
        for s in range(nseg):
            tail_ref[s * ctx:s * ctx + pad, :] = jnp.zeros((pad, c), F32)
            tail_ref[s * ctx + pad:(s + 1) * ctx, :] = prev_ref[s]

    if steps_per_seq == 1:
        load_cached_context()
    else:
        pl.when(pl.program_id(0) % steps_per_seq == 0)(load_cached_context)

    xb = x_ref[...].astype(BF16)
    group = lambda j: _dot(xb, w_ref[:, j * c:(j + 1) * c])
    h = group(0) * jax.nn.sigmoid(group(1))
    for s in range(nseg):
        hp_ref[s * stride:s * stride + ctx, :] = tail_ref[s * ctx:(s + 1) * ctx, :]
        hp_ref[s * stride + ctx:(s + 1) * stride, :] = h[s * seg:(s + 1) * seg, :]

    dh = k_ref.shape[1]

    def qkv_tile(j, l0):
        lanes = slice(l0, l0 + MXU_ROWS)
        r = _dot(xb, w_ref[:, j * c + l0:j * c + l0 + MXU_ROWS])
        if j == 2:
            q_ref[:, lanes] = r.astype(BF16)
            return
        state_ref, bf_ref = (k_ref, kb_ref) if j == 3 else (v_ref, vb_ref)
        for i in range(MXU_ROWS // dh):
            state_ref[pl.ds(l0 // dh + i, tm, stride=SB_HEADS), :] = r[:, i * dh:(i + 1) * dh]
        bf_ref[:, lanes] = r.astype(BF16)

    qkv_tiles = [(j, l0) for j in (2, 3, 4) for l0 in range(0, c, MXU_ROWS)]

    span = nseg * stride - sub
    for r in range(1, sub):
        sh_ref[r - 1, 0:span, :] = hp_ref[r:r + span, :]

    for tile in qkv_tiles:
        qkv_tile(*tile)

    groups = rc // sub
    for s in range(nseg):
        for ci in range(seg // rc):
            r0 = s * stride + ci * rc
            acc = [jnp.broadcast_to(cb_ref[...], (sub, c)) for _ in range(groups)]
            for j in range(CONV_WIDTH):
                a, r = divmod(pad + j, sub)
                wj = cw_ref[j]
                for gi in range(groups):
                    lo = r0 + (a + gi) * sub
                    rows = hp_ref[lo:lo + sub, :] if r == 0 else sh_ref[r - 1, lo:lo + sub, :]
                    acc[gi] = acc[gi] + wj * rows
            y = _layer_norm(jnp.concatenate(acc, axis=0), g_ref[...], b_ref[...])
            y_ref[s * seg + ci * rc:s * seg + (ci + 1) * rc, :] = (y * jax.nn.sigmoid(y)).astype(y_ref.dtype)

    for s in range(nseg):
        last = hp_ref[s * stride + seg:(s + 1) * stride, :]
        tail_ref[s * ctx:(s + 1) * ctx, :] = last
        st_ref[s] = last[pad:, :]


def _in_proj_conv(x, w_in, prev, conv_w, conv_b, ln_g, ln_b, *, tm, seg, steps_per_seq):
    n, d = x.shape
    cols = w_in.shape[1]
    c = cols // 5
    dh = c // SB_HEADS
    nseg = tm // seg
    nreq = prev.shape[0]
    hp_rows = nseg * (CONV_CTX_ROWS + seg)
    out_block = pl.BlockSpec((tm, c), lambda i: (i, 0))
    state_block = pl.BlockSpec((tm * SB_HEADS, dh), lambda i: (i, 0))
    conv_state_block = pl.BlockSpec((nseg, CONV_STATE, c), lambda i: (i // steps_per_seq, 0, 0))
    vec = pl.BlockSpec((1, c), lambda i: (0, 0))
    return pl.pallas_call(
        functools.partial(_in_proj_conv_kernel, seg=seg, steps_per_seq=steps_per_seq),
        out_shape=(
            jax.ShapeDtypeStruct((n, c), BF16),
            jax.ShapeDtypeStruct((nreq, CONV_STATE, c), F32),
            jax.ShapeDtypeStruct((n, c), BF16),
            jax.ShapeDtypeStruct((n * SB_HEADS, dh), F32),
            jax.ShapeDtypeStruct((n * SB_HEADS, dh), F32),
            jax.ShapeDtypeStruct((n, c), BF16),
            jax.ShapeDtypeStruct((n, c), BF16),
        ),
        grid=(n // tm,),
        in_specs=[
            pl.BlockSpec((tm, d), lambda i: (i, 0)),
            pl.BlockSpec((d, cols), lambda i: (0, 0), pipeline_mode=pl.Buffered(1)),
            conv_state_block,
            pl.BlockSpec((CONV_WIDTH, CONV_SUBLANES, c), lambda i: (0, 0, 0)),
            vec, vec, vec,
        ],
        out_specs=(out_block, conv_state_block, out_block, state_block, state_block, out_block, out_block),
        scratch_shapes=[pltpu.VMEM((nseg * CONV_CTX_ROWS, c), F32),
                        pltpu.VMEM((hp_rows, c), F32),
                        pltpu.VMEM((CONV_SUBLANES - 1, hp_rows - CONV_SUBLANES, c), F32)],
        compiler_params=_params("arbitrary"),
        name="in_proj_conv",
    )(x, w_in, prev, jnp.broadcast_to(conv_w[:, None, :], (CONV_WIDTH, CONV_SUBLANES, c)), conv_b, ln_g, ln_b)


def _sb_blocks(qs, ks, vs, tri, state, mask, scale):
    n = len(qs)
    tq = qs[0].shape[0]
    zs = [_dot_nt(qs[a], ks[a]) * scale for a in range(n)]
    sps = []
    for z in zs:
        sp = jnp.maximum(z, 0.0) + jnp.log(1.0 + jnp.exp2(jnp.abs(z) * -LOG2_E))
        sps.append(sp if mask is None else jnp.where(mask, sp, 0.0))
    parts = []
    for sp in sps:
        hi = sp.astype(BF16)
        parts += [hi, (sp - hi.astype(F32)).astype(BF16)]
    sums = _dot(jnp.concatenate(parts, axis=0), tri)
    out = []
    for a in range(n):
        carry, acc = state[a]
        later = sums[2 * a * tq:(2 * a + 1) * tq] + sums[(2 * a + 1) * tq:(2 * a + 2) * tq]
        w = jnp.exp(zs[a] - sps[a] - later - carry)
        if mask is not None:
            w = jnp.where(mask, w, 0.0)
        out.append((carry + jnp.sum(sps[a], axis=1, keepdims=True), acc + _dot(w.astype(BF16), vs[a])))
    return tuple(out)


def _causal_mask(n):
    row = lax.broadcasted_iota(jnp.int32, (n, n), 0)
    col = lax.broadcasted_iota(jnp.int32, (n, n), 1)
    return col < row


def _sb_prompt_kernel(q_ref, k_ref, v_ref, tri_ref, o_ref, *, tq, dh, scale):
    qi = pl.program_id(2)
    heads = q_ref.shape[1] // dh
    tri = tri_ref[...]
    cols = [slice(a * dh, (a + 1) * dh) for a in range(heads)]
    qs = [q_ref[:, c] for c in cols]

    def visit(s0, state, mask):
        ks = [k_ref[0, pl.ds(s0, tq), c] for c in cols]
        vs = [v_ref[0, pl.ds(s0, tq), c] for c in cols]
        return _sb_blocks(qs, ks, vs, tri, state, mask, scale)

    state = tuple((jnp.zeros((tq, 1), F32), jnp.zeros((tq, dh), F32)) for _ in range(heads))
    state = visit(pl.multiple_of(qi * tq, tq), state, _causal_mask(tq))

    def live(st):
        lowest = functools.reduce(jnp.minimum, [jnp.min(c) for c, _ in st])
        return lowest < SB_CARRY_CUTOFF

    def cond(loop):
        i, more, _ = loop
        return jnp.logical_and(i < qi, more)

    def body(loop):
        i, _, st = loop
        st = visit(pl.multiple_of((qi - 1 - i) * tq, tq), st, None)
        return i + 1, live(st), st

    state = lax.while_loop(cond, body, (jnp.int32(0), live(state), state))[2]
    for a in range(heads):
        o_ref[:, cols[a]] = state[a][1].astype(o_ref.dtype)


def _sb_attention_prompt(q, kb, vb, tri, *, bsz, t, tq, heads_per_step):
    n, c = q.shape
    dh = c // SB_HEADS
    nq = t // tq
    w = heads_per_step * dh
    kv_spec = pl.BlockSpec((1, t, w), lambda b, g, i: (b, 0, g))
    return pl.pallas_call(
        functools.partial(_sb_prompt_kernel, tq=tq, dh=dh, scale=dh ** -0.5),
        out_shape=jax.ShapeDtypeStruct((n, c), BF16),
        grid=(bsz, SB_HEADS // heads_per_step, nq),
        in_specs=[
            pl.BlockSpec((tq, w), lambda b, g, i: (b * nq + i, g)),
            kv_spec, kv_spec,
            pl.BlockSpec((tq, tq), lambda b, g, i: (0, 0)),
        ],
        out_specs=pl.BlockSpec((tq, w), lambda b, g, i: (b * nq + i, g)),
        compiler_params=_params("parallel", "parallel", "arbitrary"),
        name="sb_attention_prompt",
    )(q, kb.reshape(bsz, t, c), vb.reshape(bsz, t, c), tri)


def _sb_sample_kernel(q_ref, kn_ref, vn_ref, kp_ref, vp_ref, trin_ref, trip_ref, o_ref, *, tk, dh, scale):
    tq = q_ref.shape[0]
    cols = [slice(a * dh, (a + 1) * dh) for a in range(SB_HEADS)]
    qs = [q_ref[:, c] for c in cols]
    tri_new = trin_ref[...]
    tri_past = trip_ref[...]
    state = tuple((jnp.zeros((tq, 1), F32), jnp.zeros((tq, dh), F32)) for _ in range(SB_HEADS))
    state = _sb_blocks(qs, [kn_ref[:, c] for c in cols], [vn_ref[:, c] for c in cols], tri_new,
                       state, _causal_mask(tq), scale)
    n_past = kp_ref.shape[1] // (tk * SB_HEADS)
    for i in range(n_past):
        s0 = (n_past - 1 - i) * tk * SB_HEADS
        rows = [pl.ds(s0 + a, tk, stride=SB_HEADS) for a in range(SB_HEADS)]
        state = _sb_blocks(qs, [kp_ref[0, r, :].astype(BF16) for r in rows],
                           [vp_ref[0, r, :].astype(BF16) for r in rows], tri_past, state, None, scale)
    for a in range(SB_HEADS):
        o_ref[:, cols[a]] = state[a][1].astype(o_ref.dtype)


def _sb_attention_sample(q, kb_new, vb_new, k_past, v_past, tri_new, tri_past, *, bsz, t, tk):
    n, c = q.shape
    dh = c // SB_HEADS
    rows_past = k_past.shape[1]
    new_spec = pl.BlockSpec((t, c), lambda b: (b, 0))
    past_spec = pl.BlockSpec((1, rows_past, dh), lambda b: (b, 0, 0))
    return pl.pallas_call(
        functools.partial(_sb_sample_kernel, tk=tk, dh=dh, scale=dh ** -0.5),
        out_shape=jax.ShapeDtypeStruct((n, c), BF16),
        grid=(bsz,),
        in_specs=[
            new_spec, new_spec, new_spec, past_spec, past_spec,
            pl.BlockSpec((t, t), lambda b: (0, 0)),
            pl.BlockSpec((tk, tk), lambda b: (0, 0)),
        ],
        out_specs=new_spec,
        compiler_params=_params("parallel"),
        name="sb_attention_sample",
    )(q, kb_new, vb_new, k_past, v_past, tri_new, tri_past)


def _strict_lower_ones(n):
    idx = jnp.arange(n)
    return (idx[:, None] > idx[None, :]).astype(BF16)


def _out_proj_kernel(x_ref, yc_ref, ys_ref, wc_ref, ws_ref, g_ref, b_ref, o_ref, *, alpha):
    for rows in _row_parts(x_ref.shape[0]):
        mix = _dot(yc_ref[rows, :], wc_ref[...]) + _dot(ys_ref[rows, :], ws_ref[...])
        o_ref[rows, :] = _layer_norm(alpha * x_ref[rows, :] + mix, g_ref[...], b_ref[...])


def _out_proj_post_norm(x, y_conv, y_sb, w_out, g, b, *, alpha, tm):
    n, d = x.shape
    c = y_conv.shape[1]
    vec = pl.BlockSpec((1, d), lambda i: (0, 0))
    return pl.pallas_call(
        functools.partial(_out_proj_kernel, alpha=alpha),
        out_shape=jax.ShapeDtypeStruct((n, d), F32),
        grid=(n // tm,),
        in_specs=[
            pl.BlockSpec((tm, d), lambda i: (i, 0)),
            pl.BlockSpec((tm, c), lambda i: (i, 0)),
            pl.BlockSpec((tm, c), lambda i: (i, 0)),
            pl.BlockSpec((c, d), lambda i: (0, 0), pipeline_mode=pl.Buffered(1)),
            pl.BlockSpec((c, d), lambda i: (1, 0), pipeline_mode=pl.Buffered(1)),
            vec, vec,
        ],
        out_specs=pl.BlockSpec((tm, d), lambda i: (i, 0)),
        compiler_params=_params("parallel"),
        name="out_proj_post_norm",
    )(x, y_conv, y_sb, w_out, w_out, g, b)


def _mem_kv_kernel(m_ref, wk_ref, wv_ref, k_ref, v_ref, ks_ref, vs_ref):
    mb = m_ref[...].astype(BF16)
    nreq, m, heads, dh = ks_ref.shape
    for w_ref, flat_ref, state_ref in ((wk_ref, k_ref, ks_ref), (wv_ref, v_ref, vs_ref)):
        r = _dot(mb, w_ref[...])
        flat_ref[...] = r
        for i in range(nreq):
            for h in range(heads):
                state_ref[i, :, h, :] = r[i * m:(i + 1) * m, h * dh:(h + 1) * dh]


def _memory_kv(mem, wk, wv, *, tm):
    bsz, m, d = mem.shape
    c = wk.shape[1]
    dh = c // XATTN_HEADS
    w_spec = pl.BlockSpec((d, c), lambda i: (0, 0))
    flat_spec = pl.BlockSpec((tm, c), lambda i: (i, 0))
    state_spec = pl.BlockSpec((tm // m, m, XATTN_HEADS, dh), lambda i: (i, 0, 0, 0))
    flat = jax.ShapeDtypeStruct((bsz * m, c), F32)
    state = jax.ShapeDtypeStruct((bsz, m, XATTN_HEADS, dh), F32)
    return pl.pallas_call(
        _mem_kv_kernel,
        out_shape=(flat, flat, state, state),
        grid=(bsz * m // tm,),
        in_specs=[pl.BlockSpec((tm, d), lambda i: (i, 0)), w_spec, w_spec],
        out_specs=(flat_spec, flat_spec, state_spec, state_spec),
        compiler_params=_params("parallel"),
        name="memory_kv",
    )(mem.reshape(bsz * m, d), wk, wv)


def _xattn_kernel(x_ref, mk_ref, mv_ref, wq_ref, wo_ref, g_ref, b_ref, o_ref, *, alpha, scale):
    nreq = mk_ref.shape[0]
    dh = mk_ref.shape[2] // XATTN_HEADS
    mem = [(mk_ref[i].astype(BF16), mv_ref[i].astype(BF16)) for i in range(nreq)]
    for rows in (_row_parts(x_ref.shape[0]) if nreq == 1 else [slice(0, x_ref.shape[0])]):
        x = x_ref[rows, :]
        q = _dot(x.astype(BF16), wq_ref[...])
        seg = q.shape[0] // nreq
        per_request = []
        for i, (mk, mv) in enumerate(mem):
            heads = []
            for h in range(XATTN_HEADS):
                sl = slice(h * dh, (h + 1) * dh)
                s = _dot_nt(q[i * seg:(i + 1) * seg, sl].astype(BF16), mk[:, sl]) * scale
                e = jnp.exp(s - jnp.max(s, axis=-1, keepdims=True))
                p = e / jnp.sum(e, axis=-1, keepdims=True)
                heads.append(_dot(p.astype(BF16), mv[:, sl]).astype(BF16))
            per_request.append(jnp.concatenate(heads, axis=1))
        o = jnp.concatenate(per_request, axis=0)
        y = alpha * x + _dot(o, wo_ref[...])
        o_ref[rows, :] = _layer_norm(y, g_ref[...], b_ref[...])


def _xattn_post_norm(x, mem_k, mem_v, wq, wo, g, b, *, alpha, t, tm):
    n, d = x.shape
    m, c = mem_k.shape[1], mem_k.shape[2]
    nreq = max(1, tm // t)
    tiles_per_request = max(1, t // tm)
    vec = pl.BlockSpec((1, d), lambda i: (0, 0))
    mem_spec = pl.BlockSpec((nreq, m, c), lambda i: (i // tiles_per_request, 0, 0))
    return pl.pallas_call(
        functools.partial(_xattn_kernel, alpha=alpha, scale=(c // XATTN_HEADS) ** -0.5),
        out_shape=jax.ShapeDtypeStruct((n, d), F32),
        grid=(n // tm,),
        in_specs=[
            pl.BlockSpec((tm, d), lambda i: (i, 0)),
            mem_spec, mem_spec,
            pl.BlockSpec((d, c), lambda i: (0, 0), pipeline_mode=pl.Buffered(1)),
            pl.BlockSpec((c, d), lambda i: (0, 0), pipeline_mode=pl.Buffered(1)),
            vec, vec,
        ],
        out_specs=pl.BlockSpec((tm, d), lambda i: (i, 0)),
        compiler_params=_params("parallel"),
        name="xattn_post_norm",
    )(x, mem_k, mem_v, wq, wo, g, b)


LATER_MATRICES = ("w_in", "w_out", "xq", "xo", "f2g", "f2u", "f2d")


def _encoder_layer(x3, conv_prev, sb_past, mem_k, mem_v, w, *, alpha):
    bsz, t, d = x3.shape
    x = x3.reshape(bsz * t, d)
    rows = bsz * t
    tq = SB_BLOCK
    pending = [name for name in LATER_MATRICES if w[name].dtype != BF16]
    ffn_tm = min(bsz * t, FFN_ROWS)
    x, cast = _ffn_post_norm(x, w["f1g"], w["f1u"], w["f1d"], w["ln1g"], w["ln1b"], alpha=alpha, tm=ffn_tm,
                             tf=FFN_HIDDEN_TILE,
                             cast=tuple(w[name] for name in pending))
    w = {**w, **dict(zip(pending, cast))}
    seg = min(t, IN_PROJ_ROWS)
    y_conv, conv_new, q, k, v, kb, vb = _in_proj_conv(
        x, w["w_in"], conv_prev, w["conv_w"], w["conv_b"], w["cln_g"], w["cln_b"],
        tm=IN_PROJ_ROWS, seg=seg, steps_per_seq=t // seg)
    c = y_conv.shape[1]
    if sb_past is None:
        y_sb = _sb_attention_prompt(q, kb, vb, _strict_lower_ones(tq), bsz=bsz, t=t, tq=tq,
                                    heads_per_step=SB_HEADS_PER_STEP)
    else:
        k_past, v_past = sb_past
        y_sb = _sb_attention_sample(q, kb, vb, k_past, v_past, _strict_lower_ones(t), _strict_lower_ones(tq),
                                    bsz=bsz, t=t, tk=tq)
    x = _out_proj_post_norm(x, y_conv, y_sb, w["w_out"], w["ln2g"], w["ln2b"], alpha=alpha,
                            tm=min(rows, OUT_PROJ_ROWS))
    x = _xattn_post_norm(x, mem_k, mem_v, w["xq"], w["xo"], w["ln3g"], w["ln3b"], alpha=alpha, t=t,
                         tm=min(rows, XATTN_ROWS))
    x, _ = _ffn_post_norm(x, w["f2g"], w["f2u"], w["f2d"], w["ln4g"], w["ln4b"], alpha=alpha, tm=ffn_tm,
                          tf=FFN_HIDDEN_TILE)
    dh = c // SB_HEADS
    return (x.reshape(bsz, t, d), conv_new, k.reshape(bsz, t, SB_HEADS, dh), v.reshape(bsz, t, SB_HEADS, dh)), w


def kernel(x_prompt, x_sample, mem_prompt, cache_conv, cache_sb_k, cache_sb_v, cache_mem_k, cache_mem_v,
           ffn1_w_gate, ffn1_w_up, ffn1_w_down, ln1_g, ln1_b, w_in, conv_w, conv_b, conv_ln_g, conv_ln_b,
           w_out, ln2_g, ln2_b, xattn_wq, xattn_wk, xattn_wv, xattn_wo, ln3_g, ln3_b,
           ffn2_w_gate, ffn2_w_up, ffn2_w_down, ln4_g, ln4_b):
    depth = ffn1_w_gate.shape[0]
    alpha = (2.0 * depth) ** 0.25
    bp, tp, d = x_prompt.shape
    bs, ts, _ = x_sample.shape
    conv_dim = conv_w.shape[2]
    xattn_dim = xattn_wq.shape[2]

    xp, xs = x_prompt, x_sample
    conv_p, k_p, v_p, mk_p, mv_p, conv_s, k_s, v_s = [], [], [], [], [], [], [], []
    for l in range(depth):
        row = lambda a: a[l][None, :]
        w = dict(
            f1g=ffn1_w_gate[l].astype(BF16), f1u=ffn1_w_up[l].astype(BF16), f1d=ffn1_w_down[l].astype(BF16),
            ln1g=row(ln1_g), ln1b=row(ln1_b), w_in=w_in[l],
            conv_w=conv_w[l], conv_b=row(conv_b), cln_g=row(conv_ln_g), cln_b=row(conv_ln_b),
            w_out=w_out[l], ln2g=row(ln2_g), ln2b=row(ln2_b),
            xq=xattn_wq[l], xo=xattn_wo[l], ln3g=row(ln3_g), ln3b=row(ln3_b),
            f2g=ffn2_w_gate[l], f2u=ffn2_w_up[l], f2d=ffn2_w_down[l],
            ln4g=row(ln4_g), ln4b=row(ln4_b),
        )
        m = mem_prompt.shape[1]
        mk, mv, mk_state, mv_state = _memory_kv(mem_prompt, xattn_wk[l].astype(BF16), xattn_wv[l].astype(BF16),
                                                tm=MEM_KV_ROWS)
        mk = mk.reshape(bp, m, xattn_dim)
        mv = mv.reshape(bp, m, xattn_dim)
        (xp, cp, kp, vp), w = _encoder_layer(xp, jnp.zeros((bp, CONV_STATE, conv_dim), F32), None, mk, mv, w,
                                             alpha=alpha)
        conv_p.append(cp); k_p.append(kp); v_p.append(vp)
        mk_p.append(mk_state)
        mv_p.append(mv_state)

        p_len = cache_sb_k.shape[2]
        dh = cache_sb_k.shape[4]
        sb_past = (cache_sb_k[l].reshape(bs, p_len * SB_HEADS, dh), cache_sb_v[l].reshape(bs, p_len * SB_HEADS, dh))
        (xs, cs, ks, vs), _ = _encoder_layer(xs, cache_conv[l], sb_past,
                                             cache_mem_k[l].reshape(bs, m, xattn_dim),
                                             cache_mem_v[l].reshape(bs, m, xattn_dim), w,
                                             alpha=alpha)
        conv_s.append(cs); k_s.append(ks); v_s.append(vs)
    stack = lambda parts: parts[0][None] if len(parts) == 1 else jnp.stack(parts)
    return (xp, xs, stack(conv_p), stack(k_p), stack(v_p), stack(mk_p), stack(mv_p),
            stack(conv_s), stack(k_s), stack(v_s))
```

```python
import functools

import jax
import jax.numpy as jnp
from jax import lax
from jax.experimental import pallas as pl
from jax.experimental.pallas import tpu as pltpu

F32 = jnp.float32
BF16 = jnp.bfloat16

LN_EPS = 1e-5
LOG2_E = 1.4426950408889634
CONV_WIDTH = 31
CONV_STATE = CONV_WIDTH - 1
SB_HEADS = 8
XATTN_HEADS = 4
CONV_CTX_ROWS = 32
CONV_ROW_CHUNK = 32
CONV_SUBLANES = 8
MXU_ROWS = 256
CAST_SLAB_ROWS = 16
IN_PROJ_ROWS = 256
SB_CARRY_CUTOFF = 110.0
VMEM_LIMIT_BYTES = 56 * 1024 * 1024
FFN_VMEM_LIMIT_BYTES = 60 * 1024 * 1024
FFN_ROWS = 1024
FFN_HIDDEN_TILE = 512
OUT_PROJ_ROWS = 512
XATTN_ROWS = 1024
MEM_KV_ROWS = 512
SB_BLOCK = 256
SB_HEADS_PER_STEP = 8


def _layer_norm(y, g, b):
    mu = jnp.mean(y, axis=-1, keepdims=True)
    d = y - mu
    var = jnp.mean(d * d, axis=-1, keepdims=True)
    return d * lax.rsqrt(var + LN_EPS) * g + b


def _dot(a, b):
    return jnp.dot(a, b, preferred_element_type=F32)


def _dot_nt(a, b):
    return lax.dot_general(a, b, (((1,), (1,)), ((), ())), preferred_element_type=F32)


def _row_parts(tm):
    parts = 2 if tm % (2 * MXU_ROWS) == 0 else 1
    return [slice(r * tm // parts, (r + 1) * tm // parts) for r in range(parts)]


def _params(*sem, flags=None):
    return pltpu.CompilerParams(dimension_semantics=sem, vmem_limit_bytes=VMEM_LIMIT_BYTES, flags=flags)


def _ffn_kernel(x_ref, wg_ref, wu_ref, wd_ref, g_ref, b_ref, *rest, alpha, nf, n_cast):
    cast_in, o_ref, cast_out, acc_ref = rest[:n_cast], rest[n_cast], rest[n_cast + 1:-1], rest[-1]
    f = pl.program_id(1)

    def step(first, last):
        for src, dst in zip(cast_in, cast_out):
            dst[...] = src[...].astype(BF16)
        tm = x_ref.shape[0]
        parts = max(1, tm // MXU_ROWS) if last else 1
        for r in range(parts):
            rows = slice(r * tm // parts, (r + 1) * tm // parts)
            x = x_ref[rows, :]
            xb = x.astype(BF16)
            gate = _dot(xb, wg_ref[...])
            up = _dot(xb, wu_ref[...])
            h = (gate * jax.nn.sigmoid(gate) * up).astype(BF16)
            total = _dot(h, wd_ref[...])
            if not first:
                total = acc_ref[rows, :] + total
            if last:
                o_ref[rows, :] = _layer_norm(alpha * x + 0.5 * total, g_ref[...], b_ref[...])
            else:
                acc_ref[rows, :] = total

    if nf == 1:
        step(True, True)
    else:
        pl.when(f == 0)(lambda: step(True, False))
        pl.when(jnp.logical_and(f > 0, f < nf - 1))(lambda: step(False, False))
        pl.when(f == nf - 1)(lambda: step(False, True))


def _cast_slab_spec(rows, cols, n_tiles, nf):
    slab = CAST_SLAB_ROWS
    while rows // slab // n_tiles > nf:
        slab += CAST_SLAB_ROWS
    per_tile = rows // slab // n_tiles
    assert per_tile * slab * n_tiles == rows and per_tile >= 1, (rows, n_tiles, nf)
    return pl.BlockSpec((slab, cols), lambda i, f: (i * per_tile + jnp.minimum(f, per_tile - 1), 0))


def _ffn_post_norm(x, wg, wu, wd, g, b, *, alpha, tm, tf, cast=()):
    n, d = x.shape
    dff = wg.shape[1]
    nf = dff // tf
    cast_specs = [_cast_slab_spec(w.shape[0], w.shape[1], n // tm, nf) for w in cast]
    out = pl.pallas_call(
        functools.partial(_ffn_kernel, alpha=alpha, nf=nf, n_cast=len(cast)),
        out_shape=(jax.ShapeDtypeStruct((n, d), F32), *[jax.ShapeDtypeStruct(w.shape, BF16) for w in cast]),
        grid=(n // tm, nf),
        in_specs=[
            pl.BlockSpec((tm, d), lambda i, f: (i, 0)),
            pl.BlockSpec((d, tf), lambda i, f: (0, f)),
            pl.BlockSpec((d, tf), lambda i, f: (0, f)),
            pl.BlockSpec((tf, d), lambda i, f: (f, 0)),
            pl.BlockSpec((1, d), lambda i, f: (0, 0)),
            pl.BlockSpec((1, d), lambda i, f: (0, 0)),
            *cast_specs,
        ],
        out_specs=(pl.BlockSpec((tm, d), lambda i, f: (i, 0), pipeline_mode=pl.Buffered(1)), *cast_specs),
        scratch_shapes=[pltpu.VMEM((tm, d), F32)],
        compiler_params=pltpu.CompilerParams(dimension_semantics=("arbitrary", "arbitrary"),
                                             vmem_limit_bytes=FFN_VMEM_LIMIT_BYTES),
        name="ffn_post_norm",
    )(x, wg, wu, wd, g, b, *cast)
    return out[0], out[1:]


def _store_heads_interleaved(dst_ref, r):
    n, c = r.shape
    dh = c // SB_HEADS
    for h in range(SB_HEADS):
        dst_ref[pl.ds(h, n, stride=SB_HEADS), :] = r[:, h * dh:(h + 1) * dh]


def _in_proj_conv_kernel(x_ref, w_ref, prev_ref, cw_ref, cb_ref, g_ref, b_ref,
                         y_ref, st_ref, q_ref, k_ref, v_ref, kb_ref, vb_ref,
                         tail_ref, hp_ref, yc_ref, *, seg, steps_per_seq):
    ctx, sub, rc = CONV_CTX_ROWS, CONV_SUBLANES, CONV_ROW_CHUNK
    pad = ctx - CONV_STATE
    tm = x_ref.shape[0]
    c = y_ref.shape[1]
    nseg = tm // seg
    stride = ctx + seg

    def load_cached_context():
        for s in range(nseg):
            tail_ref[s * ctx:s * ctx + pad, :] = jnp.zeros((pad, c), F32)
            tail_ref[s * ctx + pad:(s + 1) * ctx, :] = prev_ref[s]

    if steps_per_seq == 1:
        load_cached_context()
    else:
        pl.when(pl.program_id(0) % steps_per_seq == 0)(load_cached_context)

    xb = x_ref[...].astype(BF16)
    group = lambda j: _dot(xb, w_ref[:, j * c:(j + 1) * c])
    h = group(0) * jax.nn.sigmoid(group(1))
    lanes = c // sub

    def to_frames(dst_ref, p0, rows_val):
        for b in range(sub):
            dst_ref[pl.ds(p0 * sub + b, rows_val.shape[0], stride=sub), :] = rows_val[:, b * lanes:(b + 1) * lanes]

    def from_frames(src_ref, p0, n):
        return jnp.concatenate([src_ref[pl.ds(p0 * sub + b, n, stride=sub), :] for b in range(sub)], axis=1)

    for s in range(nseg):
        to_frames(hp_ref, s * stride, tail_ref[s * ctx:(s + 1) * ctx, :])
        to_frames(hp_ref, s * stride + ctx, h[s * seg:(s + 1) * seg, :])

    dh = k_ref.shape[1]

    def qkv_tile(j, l0):
        lanes = slice(l0, l0 + MXU_ROWS)
        r = _dot(xb, w_ref[:, j * c + l0:j * c + l0 + MXU_ROWS])
        if j == 2:
            q_ref[:, lanes] = r.astype(BF16)
            return
        state_ref, bf_ref = (k_ref, kb_ref) if j == 3 else (v_ref, vb_ref)
        for i in range(MXU_ROWS // dh):
            state_ref[pl.ds(l0 // dh + i, tm, stride=SB_HEADS), :] = r[:, i * dh:(i + 1) * dh]
        bf_ref[:, lanes] = r.astype(BF16)

    qkv_tiles = [(j, l0) for j in (2, 3, 4) for l0 in range(0, c, MXU_ROWS)]

    for tile in qkv_tiles:
        qkv_tile(*tile)

    frame = lambda ref, p: ref[p * sub:(p + 1) * sub, :]
    for s in range(nseg):
        for u0 in range(0, seg, sub):
            acc = [cb_ref[...] for _ in range(sub)]
            for j in range(CONV_WIDTH):
                wj = cw_ref[j]
                for u in range(sub):
                    acc[u] = acc[u] + wj * frame(hp_ref, s * stride + u0 + u + pad + j)
            for u in range(sub):
                yc_ref[(s * seg + u0 + u) * sub:(s * seg + u0 + u + 1) * sub, :] = acc[u]

    for r0 in range(0, tm, rc):
        y = _layer_norm(from_frames(yc_ref, r0, rc), g_ref[...], b_ref[...])
        y_ref[r0:r0 + rc, :] = (y * jax.nn.sigmoid(y)).astype(y_ref.dtype)

    for s in range(nseg):
        last = from_frames(hp_ref, s * stride + seg, ctx)
        tail_ref[s * ctx:(s + 1) * ctx, :] = last
        st_ref[s] = last[pad:, :]


def _in_proj_conv(x, w_in, prev, conv_w, conv_b, ln_g, ln_b, *, tm, seg, steps_per_seq):
    n, d = x.shape
    cols = w_in.shape[1]
    c = cols // 5
    dh = c // SB_HEADS
    nseg = tm // seg
    nreq = prev.shape[0]
    hp_rows = nseg * (CONV_CTX_ROWS + seg)
    lanes = c // CONV_SUBLANES
    out_block = pl.BlockSpec((tm, c), lambda i: (i, 0))
    state_block = pl.BlockSpec((tm * SB_HEADS, dh), lambda i: (i, 0))
    conv_state_block = pl.BlockSpec((nseg, CONV_STATE, c), lambda i: (i // steps_per_seq, 0, 0))
    vec = pl.BlockSpec((1, c), lambda i: (0, 0))
    return pl.pallas_call(
        functools.partial(_in_proj_conv_kernel, seg=seg, steps_per_seq=steps_per_seq),
        out_shape=(
            jax.ShapeDtypeStruct((n, c), BF16),
            jax.ShapeDtypeStruct((nreq, CONV_STATE, c), F32),
            jax.ShapeDtypeStruct((n, c), BF16),
            jax.ShapeDtypeStruct((n * SB_HEADS, dh), F32),
            jax.ShapeDtypeStruct((n * SB_HEADS, dh), F32),
            jax.ShapeDtypeStruct((n, c), BF16),
            jax.ShapeDtypeStruct((n, c), BF16),
        ),
        grid=(n // tm,),
        in_specs=[
            pl.BlockSpec((tm, d), lambda i: (i, 0)),
            pl.BlockSpec((d, cols), lambda i: (0, 0), pipeline_mode=pl.Buffered(1)),
            conv_state_block,
            pl.BlockSpec((CONV_WIDTH, CONV_SUBLANES, lanes), lambda i: (0, 0, 0)),
            pl.BlockSpec((CONV_SUBLANES, lanes), lambda i: (0, 0)),
            vec, vec,
        ],
        out_specs=(out_block, conv_state_block, out_block, state_block, state_block, out_block, out_block),
        scratch_shapes=[pltpu.VMEM((nseg * CONV_CTX_ROWS, c), F32),
                        pltpu.VMEM((hp_rows * CONV_SUBLANES, lanes), F32),
                        pltpu.VMEM((tm * CONV_SUBLANES, lanes), F32)],
        compiler_params=_params("arbitrary"),
        name="in_proj_conv",
    )(x, w_in, prev, conv_w.reshape(CONV_WIDTH, CONV_SUBLANES, lanes), conv_b.reshape(CONV_SUBLANES, lanes),
      ln_g, ln_b)


def _sb_blocks(qs, ks, vs, tri, state, mask, scale):
    n = len(qs)
    tq = qs[0].shape[0]
    zs = [_dot_nt(qs[a], ks[a]) * scale for a in range(n)]
    sps = []
    for z in zs:
        sp = jnp.maximum(z, 0.0) + jnp.log(1.0 + jnp.exp2(jnp.abs(z) * -LOG2_E))
        sps.append(sp if mask is None else jnp.where(mask, sp, 0.0))
    parts = []
    for sp in sps:
        hi = sp.astype(BF16)
        parts += [hi, (sp - hi.astype(F32)).astype(BF16)]
    sums = _dot(jnp.concatenate(parts, axis=0), tri)
    out = []
    for a in range(n):
        carry, acc = state[a]
        later = sums[2 * a * tq:(2 * a + 1) * tq] + sums[(2 * a + 1) * tq:(2 * a + 2) * tq]
        w = jnp.exp(zs[a] - sps[a] - later - carry)
        if mask is not None:
            w = jnp.where(mask, w, 0.0)
        out.append((carry + jnp.sum(sps[a], axis=1, keepdims=True), acc + _dot(w.astype(BF16), vs[a])))
    return tuple(out)


def _causal_mask(n):
    row = lax.broadcasted_iota(jnp.int32, (n, n), 0)
    col = lax.broadcasted_iota(jnp.int32, (n, n), 1)
    return col < row


def _sb_prompt_kernel(q_ref, k_ref, v_ref, tri_ref, o_ref, *, tq, dh, scale):
    qi = pl.program_id(2)
    heads = q_ref.shape[1] // dh
    tri = tri_ref[...]
    cols = [slice(a * dh, (a + 1) * dh) for a in range(heads)]
    qs = [q_ref[:, c] for c in cols]

    def visit(s0, state, mask):
        ks = [k_ref[0, pl.ds(s0, tq), c] for c in cols]
        vs = [v_ref[0, pl.ds(s0, tq), c] for c in cols]
        return _sb_blocks(qs, ks, vs, tri, state, mask, scale)

    state = tuple((jnp.zeros((tq, 1), F32), jnp.zeros((tq, dh), F32)) for _ in range(heads))
    state = visit(pl.multiple_of(qi * tq, tq), state, _causal_mask(tq))

    def live(st):
        lowest = functools.reduce(jnp.minimum, [jnp.min(c) for c, _ in st])
        return lowest < SB_CARRY_CUTOFF

    def cond(loop):
        i, more, _ = loop
        return jnp.logical_and(i < qi, more)

    def body(loop):
        i, _, st = loop
        st = visit(pl.multiple_of((qi - 1 - i) * tq, tq), st, None)
        return i + 1, live(st), st

    state = lax.while_loop(cond, body, (jnp.int32(0), live(state), state))[2]
    for a in range(heads):
        o_ref[:, cols[a]] = state[a][1].astype(o_ref.dtype)


def _sb_attention_prompt(q, kb, vb, tri, *, bsz, t, tq, heads_per_step):
    n, c = q.shape
    dh = c // SB_HEADS
    nq = t // tq
    w = heads_per_step * dh
    kv_spec = pl.BlockSpec((1, t, w), lambda b, g, i: (b, 0, g))
    return pl.pallas_call(
        functools.partial(_sb_prompt_kernel, tq=tq, dh=dh, scale=dh ** -0.5),
        out_shape=jax.ShapeDtypeStruct((n, c), BF16),
        grid=(bsz, SB_HEADS // heads_per_step, nq),
        in_specs=[
            pl.BlockSpec((tq, w), lambda b, g, i: (b * nq + i, g)),
            kv_spec, kv_spec,
            pl.BlockSpec((tq, tq), lambda b, g, i: (0, 0)),
        ],
        out_specs=pl.BlockSpec((tq, w), lambda b, g, i: (b * nq + i, g)),
        compiler_params=_params("parallel", "parallel", "arbitrary"),
        name="sb_attention_prompt",
    )(q, kb.reshape(bsz, t, c), vb.reshape(bsz, t, c), tri)


def _sb_sample_kernel(q_ref, kn_ref, vn_ref, kp_ref, vp_ref, trin_ref, trip_ref, o_ref, *, tk, dh, scale):
    tq = q_ref.shape[0]
    cols = [slice(a * dh, (a + 1) * dh) for a in range(SB_HEADS)]
    qs = [q_ref[:, c] for c in cols]
    tri_new = trin_ref[...]
    tri_past = trip_ref[...]
    state = tuple((jnp.zeros((tq, 1), F32), jnp.zeros((tq, dh), F32)) for _ in range(SB_HEADS))
    state = _sb_blocks(qs, [kn_ref[:, c] for c in cols], [vn_ref[:, c] for c in cols], tri_new,
                       state, _causal_mask(tq), scale)
    n_past = kp_ref.shape[1] // (tk * SB_HEADS)
    for i in range(n_past):
        s0 = (n_past - 1 - i) * tk * SB_HEADS
        rows = [pl.ds(s0 + a, tk, stride=SB_HEADS) for a in range(SB_HEADS)]
        state = _sb_blocks(qs, [kp_ref[0, r, :].astype(BF16) for r in rows],
                           [vp_ref[0, r, :].astype(BF16) for r in rows], tri_past, state, None, scale)
    for a in range(SB_HEADS):
        o_ref[:, cols[a]] = state[a][1].astype(o_ref.dtype)


def _sb_attention_sample(q, kb_new, vb_new, k_past, v_past, tri_new, tri_past, *, bsz, t, tk):
    n, c = q.shape
    dh = c // SB_HEADS
    rows_past = k_past.shape[1]
    new_spec = pl.BlockSpec((t, c), lambda b: (b, 0))
    past_spec = pl.BlockSpec((1, rows_past, dh), lambda b: (b, 0, 0))
    return pl.pallas_call(
        functools.partial(_sb_sample_kernel, tk=tk, dh=dh, scale=dh ** -0.5),
        out_shape=jax.ShapeDtypeStruct((n, c), BF16),
        grid=(bsz,),
        in_specs=[
            new_spec, new_spec, new_spec, past_spec, past_spec,
            pl.BlockSpec((t, t), lambda b: (0, 0)),
            pl.BlockSpec((tk, tk), lambda b: (0, 0)),
        ],
        out_specs=new_spec,
        compiler_params=_params("parallel"),
        name="sb_attention_sample",
    )(q, kb_new, vb_new, k_past, v_past, tri_new, tri_past)


def _strict_lower_ones(n):
    idx = jnp.arange(n)
    return (idx[:, None] > idx[None, :]).astype(BF16)


def _out_proj_kernel(x_ref, yc_ref, ys_ref, wc_ref, ws_ref, g_ref, b_ref, o_ref, *, alpha):
    for rows in _row_parts(x_ref.shape[0]):
        mix = _dot(yc_ref[rows, :], wc_ref[...]) + _dot(ys_ref[rows, :], ws_ref[...])
        o_ref[rows, :] = _layer_norm(alpha * x_ref[rows, :] + mix, g_ref[...], b_ref[...])


def _out_proj_post_norm(x, y_conv, y_sb, w_out, g, b, *, alpha, tm):
    n, d = x.shape
    c = y_conv.shape[1]
    vec = pl.BlockSpec((1, d), lambda i: (0, 0))
    return pl.pallas_call(
        functools.partial(_out_proj_kernel, alpha=alpha),
        out_shape=jax.ShapeDtypeStruct((n, d), F32),
        grid=(n // tm,),
        in_specs=[
            pl.BlockSpec((tm, d), lambda i: (i, 0)),
            pl.BlockSpec((tm, c), lambda i: (i, 0)),
            pl.BlockSpec((tm, c), lambda i: (i, 0)),
            pl.BlockSpec((c, d), lambda i: (0, 0), pipeline_mode=pl.Buffered(1)),
            pl.BlockSpec((c, d), lambda i: (1, 0), pipeline_mode=pl.Buffered(1)),
            vec, vec,
        ],
        out_specs=pl.BlockSpec((tm, d), lambda i: (i, 0)),
        compiler_params=_params("parallel"),
        name="out_proj_post_norm",
    )(x, y_conv, y_sb, w_out, w_out, g, b)


def _mem_kv_kernel(m_ref, wk_ref, wv_ref, k_ref, v_ref, ks_ref, vs_ref):
    mb = m_ref[...].astype(BF16)
    nreq, m, heads, dh = ks_ref.shape
    for w_ref, flat_ref, state_ref in ((wk_ref, k_ref, ks_ref), (wv_ref, v_ref, vs_ref)):
        r = _dot(mb, w_ref[...])
        flat_ref[...] = r
        for i in range(nreq):
            for h in range(heads):
                state_ref[i, :, h, :] = r[i * m:(i + 1) * m, h * dh:(h + 1) * dh]


def _memory_kv(mem, wk, wv, *, tm):
    bsz, m, d = mem.shape
    c = wk.shape[1]
    dh = c // XATTN_HEADS
    w_spec = pl.BlockSpec((d, c), lambda i: (0, 0))
    flat_spec = pl.BlockSpec((tm, c), lambda i: (i, 0))
    state_spec = pl.BlockSpec((tm // m, m, XATTN_HEADS, dh), lambda i: (i, 0, 0, 0))
    flat = jax.ShapeDtypeStruct((bsz * m, c), F32)
    state = jax.ShapeDtypeStruct((bsz, m, XATTN_HEADS, dh), F32)
    return pl.pallas_call(
        _mem_kv_kernel,
        out_shape=(flat, flat, state, state),
        grid=(bsz * m // tm,),
        in_specs=[pl.BlockSpec((tm, d), lambda i: (i, 0)), w_spec, w_spec],
        out_specs=(flat_spec, flat_spec, state_spec, state_spec),
        compiler_params=_params("parallel"),
        name="memory_kv",
    )(mem.reshape(bsz * m, d), wk, wv)


def _xattn_kernel(x_ref, mk_ref, mv_ref, wq_ref, wo_ref, g_ref, b_ref, o_ref, *, alpha, scale):
    nreq = mk_ref.shape[0]
    dh = mk_ref.shape[2] // XATTN_HEADS
    mem = [(mk_ref[i].astype(BF16), mv_ref[i].astype(BF16)) for i in range(nreq)]
    for rows in (_row_parts(x_ref.shape[0]) if nreq == 1 else [slice(0, x_ref.shape[0])]):
        x = x_ref[rows, :]
        q = _dot(x.astype(BF16), wq_ref[...])
        seg = q.shape[0] // nreq
        per_request = []
        for i, (mk, mv) in enumerate(mem):
            heads = []
            for h in range(XATTN_HEADS):
                sl = slice(h * dh, (h + 1) * dh)
                s = _dot_nt(q[i * seg:(i + 1) * seg, sl].astype(BF16), mk[:, sl]) * scale
                e = jnp.exp(s - jnp.max(s, axis=-1, keepdims=True))
                p = e / jnp.sum(e, axis=-1, keepdims=True)
                heads.append(_dot(p.astype(BF16), mv[:, sl]).astype(BF16))
            per_request.append(jnp.concatenate(heads, axis=1))
        o = jnp.concatenate(per_request, axis=0)
        y = alpha * x + _dot(o, wo_ref[...])
        o_ref[rows, :] = _layer_norm(y, g_ref[...], b_ref[...])


def _xattn_post_norm(x, mem_k, mem_v, wq, wo, g, b, *, alpha, t, tm):
    n, d = x.shape
    m, c = mem_k.shape[1], mem_k.shape[2]
    nreq = max(1, tm // t)
    tiles_per_request = max(1, t // tm)
    vec = pl.BlockSpec((1, d), lambda i: (0, 0))
    mem_spec = pl.BlockSpec((nreq, m, c), lambda i: (i // tiles_per_request, 0, 0))
    return pl.pallas_call(
        functools.partial(_xattn_kernel, alpha=alpha, scale=(c // XATTN_HEADS) ** -0.5),
        out_shape=jax.ShapeDtypeStruct((n, d), F32),
        grid=(n // tm,),
        in_specs=[
            pl.BlockSpec((tm, d), lambda i: (i, 0)),
            mem_spec, mem_spec,
            pl.BlockSpec((d, c), lambda i: (0, 0), pipeline_mode=pl.Buffered(1)),
            pl.BlockSpec((c, d), lambda i: (0, 0), pipeline_mode=pl.Buffered(1)),
            vec, vec,
        ],
        out_specs=pl.BlockSpec((tm, d), lambda i: (i, 0)),
        compiler_params=_params("parallel"),
        name="xattn_post_norm",
    )(x, mem_k, mem_v, wq, wo, g, b)


LATER_MATRICES = ("w_in", "w_out", "xq", "xo", "f2g", "f2u", "f2d")


def _encoder_layer(x3, conv_prev, sb_past, mem_k, mem_v, w, *, alpha):
    bsz, t, d = x3.shape
    x = x3.reshape(bsz * t, d)
    rows = bsz * t
    tq = SB_BLOCK
    pending = [name for name in LATER_MATRICES if w[name].dtype != BF16]
    ffn_tm = min(bsz * t, FFN_ROWS)
    x, cast = _ffn_post_norm(x, w["f1g"], w["f1u"], w["f1d"], w["ln1g"], w["ln1b"], alpha=alpha, tm=ffn_tm,
                             tf=FFN_HIDDEN_TILE,
                             cast=tuple(w[name] for name in pending))
    w = {**w, **dict(zip(pending, cast))}
    seg = min(t, IN_PROJ_ROWS)
    y_conv, conv_new, q, k, v, kb, vb = _in_proj_conv(
        x, w["w_in"], conv_prev, w["conv_w"], w["conv_b"], w["cln_g"], w["cln_b"],
        tm=IN_PROJ_ROWS, seg=seg, steps_per_seq=t // seg)
    c = y_conv.shape[1]
    if sb_past is None:
        y_sb = _sb_attention_prompt(q, kb, vb, _strict_lower_ones(tq), bsz=bsz, t=t, tq=tq,
                                    heads_per_step=SB_HEADS_PER_STEP)
    else:
        k_past, v_past = sb_past
        y_sb = _sb_attention_sample(q, kb, vb, k_past, v_past, _strict_lower_ones(t), _strict_lower_ones(tq),
                                    bsz=bsz, t=t, tk=tq)
    x = _out_proj_post_norm(x, y_conv, y_sb, w["w_out"], w["ln2g"], w["ln2b"], alpha=alpha,
                            tm=min(rows, OUT_PROJ_ROWS))
    x = _xattn_post_norm(x, mem_k, mem_v, w["xq"], w["xo"], w["ln3g"], w["ln3b"], alpha=alpha, t=t,
                         tm=min(rows, XATTN_ROWS))
    x, _ = _ffn_post_norm(x, w["f2g"], w["f2u"], w["f2d"], w["ln4g"], w["ln4b"], alpha=alpha, tm=ffn_tm,
                          tf=FFN_HIDDEN_TILE)
    dh = c // SB_HEADS
    return (x.reshape(bsz, t, d), conv_new, k.reshape(bsz, t, SB_HEADS, dh), v.reshape(bsz, t, SB_HEADS, dh)), w


def kernel(x_prompt, x_sample, mem_prompt, cache_conv, cache_sb_k, cache_sb_v, cache_mem_k, cache_mem_v,
           ffn1_w_gate, ffn1_w_up, ffn1_w_down, ln1_g, ln1_b, w_in, conv_w, conv_b, conv_ln_g, conv_ln_b,
           w_out, ln2_g, ln2_b, xattn_wq, xattn_wk, xattn_wv, xattn_wo, ln3_g, ln3_b,
           ffn2_w_gate, ffn2_w_up, ffn2_w_down, ln4_g, ln4_b):
    depth = ffn1_w_gate.shape[0]
    alpha = (2.0 * depth) ** 0.25
    bp, tp, d = x_prompt.shape
    bs, ts, _ = x_sample.shape
    conv_dim = conv_w.shape[2]
    xattn_dim = xattn_wq.shape[2]

    xp, xs = x_prompt, x_sample
    conv_p, k_p, v_p, mk_p, mv_p, conv_s, k_s, v_s = [], [], [], [], [], [], [], []
    for l in range(depth):
        row = lambda a: a[l][None, :]
        w = dict(
            f1g=ffn1_w_gate[l].astype(BF16), f1u=ffn1_w_up[l].astype(BF16), f1d=ffn1_w_down[l].astype(BF16),
            ln1g=row(ln1_g), ln1b=row(ln1_b), w_in=w_in[l],
            conv_w=conv_w[l], conv_b=row(conv_b), cln_g=row(conv_ln_g), cln_b=row(conv_ln_b),
            w_out=w_out[l], ln2g=row(ln2_g), ln2b=row(ln2_b),
            xq=xattn_wq[l], xo=xattn_wo[l], ln3g=row(ln3_g), ln3b=row(ln3_b),
            f2g=ffn2_w_gate[l], f2u=ffn2_w_up[l], f2d=ffn2_w_down[l],
            ln4g=row(ln4_g), ln4b=row(ln4_b),
        )
        m = mem_prompt.shape[1]
        mk, mv, mk_state, mv_state = _memory_kv(mem_prompt, xattn_wk[l].astype(BF16), xattn_wv[l].astype(BF16),
                                                tm=MEM_KV_ROWS)
        mk = mk.reshape(bp, m, xattn_dim)
        mv = mv.reshape(bp, m, xattn_dim)
        (xp, cp, kp, vp), w = _encoder_layer(xp, jnp.zeros((bp, CONV_STATE, conv_dim), F32), None, mk, mv, w,
                                             alpha=alpha)
        conv_p.append(cp); k_p.append(kp); v_p.append(vp)
        mk_p.append(mk_state)
        mv_p.append(mv_state)

        p_len = cache_sb_k.shape[2]
        dh = cache_sb_k.shape[4]
        sb_past = (cache_sb_k[l].reshape(bs, p_len * SB_HEADS, dh), cache_sb_v[l].reshape(bs, p_len * SB_HEADS, dh))
        (xs, cs, ks, vs), _ = _encoder_layer(xs, cache_conv[l], sb_past,
                                             cache_mem_k[l].reshape(bs, m, xattn_dim),
                                             cache_mem_v[l].reshape(bs, m, xattn_dim), w,
                                             alpha=alpha)
        conv_s.append(cs); k_s.append(ks); v_s.append(vs)
    stack = lambda parts: parts[0][None] if len(parts) == 1 else jnp.stack(parts)
    return (xp, xs, stack(conv_p), stack(k_p), stack(v_p), stack(mk_p), stack(mv_p),
            stack(conv_s), stack(k_s), stack(v_s))
```
